```python
import jax
import jax.numpy as jnp
from jax import lax
import numpy as np

D_MODEL = 2048
BATCH = 4
SEQ = 4096
DEPTH = 2

GRID_W = 64
CTX_LEN = 256
EPS = 1e-6
ROPE_THETA = 10000.0
Q_BLOCK = 128

MLA_HEADS = 8
MLA_Q_RANK = 512
MLA_KV_RANK = 512
MLA_NOPE = 128
MLA_ROPE = 64
MLA_V = 128
MLA_SCALE = (MLA_NOPE + MLA_ROPE) ** -0.5

GQA_HEADS = 8
GQA_KV_HEADS = 2
HEAD_DIM = 128
HEAD_SCALE = HEAD_DIM ** -0.5

NA_HEADS = 8
NA_KH = 8
NA_KW = 16

N_BRANCH = 3
BRANCH_W = 1024

N_EXPERTS = 32
TOP_K = 4
D_FF = 2048
SWIGLU_ALPHA = 1.702
SWIGLU_LIMIT = 7.0
MOE_BLOCK = 128

MOD_SCALE = 0.5

IN_SIZES = (MLA_Q_RANK, MLA_KV_RANK, MLA_ROPE,
            GQA_HEADS * HEAD_DIM, GQA_KV_HEADS * HEAD_DIM, GQA_KV_HEADS * HEAD_DIM,
            NA_HEADS * HEAD_DIM, NA_HEADS * HEAD_DIM, NA_HEADS * HEAD_DIM,
            N_BRANCH * D_MODEL)
IN_COLS = sum(IN_SIZES)
IN_SPLITS = tuple(sum(IN_SIZES[:i + 1]) for i in range(len(IN_SIZES) - 1))

kernel_name = 'hybrid_gated_mla_gqa_natten_moe'


def rms_norm(x, g):
    xf = x.astype(jnp.float32)
    y = xf * lax.rsqrt(jnp.mean(xf * xf, axis=-1, keepdims=True) + EPS)
    return (y * g.astype(jnp.float32)).astype(x.dtype)


def modulate(x, shift, scale):
    return x * (1 + scale) + shift


def rope_1d(x, pos):
    dim = x.shape[-1]
    freqs = ROPE_THETA ** (-jnp.arange(0, dim, 2, dtype=jnp.float32) / dim)
    ang = pos[:, None, None] * freqs
    cos, sin = jnp.cos(ang), jnp.sin(ang)
    xf = x.astype(jnp.float32)
    x1, x2 = xf[..., : dim // 2], xf[..., dim // 2:]
    return jnp.concatenate([x1 * cos - x2 * sin, x2 * cos + x1 * sin], axis=-1).astype(x.dtype)


def axial_rope(x, pos_r, pos_c):
    half = x.shape[-1] // 2
    return jnp.concatenate([rope_1d(x[..., :half], pos_r), rope_1d(x[..., half:], pos_c)], axis=-1)


def block_attention(q, k, v, scale):
    B, Sq, Hq, dk = q.shape
    Hkv, dv = k.shape[2], v.shape[-1]
    G = Hq // Hkv
    nb = Sq // Q_BLOCK
    qb = q.reshape(B, nb, Q_BLOCK, Hkv, G, dk).transpose(1, 0, 2, 3, 4, 5)

    def one_block(q_blk):
        s = jnp.einsum('bqhgd,bkhd->bhgqk', q_blk, k, preferred_element_type=jnp.float32) * scale
        p = jax.nn.softmax(s, axis=-1).astype(v.dtype)
        return jnp.einsum('bhgqk,bkhe->bqhge', p, v)

    out = lax.map(one_block, qb)
    return out.transpose(1, 0, 2, 3, 4, 5).reshape(B, Sq, Hq, dv)


def neighbourhood_attention(q, k, v, k_ctx, v_ctx, rpb, scale):
    B, S, H, d = q.shape
    rows = S // GRID_W
    kh = min(NA_KH, rows)
    n_loc = kh * NA_KW
    qg = q.reshape(B, rows, GRID_W, H, d).transpose(1, 0, 2, 3, 4)
    kg = k.reshape(B, rows, GRID_W, H, d)
    vg = v.reshape(B, rows, GRID_W, H, d)
    col = jnp.arange(GRID_W, dtype=jnp.int32)
    col0 = jnp.clip(col - NA_KW // 2, 0, GRID_W - NA_KW)
    col_idx = col0[:, None] + jnp.arange(NA_KW, dtype=jnp.int32)
    rpb_cols = rpb[:, :, col_idx - col[:, None] + NA_KW - 1]

    def one_row(args):
        r, q_row = args
        r0 = jnp.clip(r - kh // 2, 0, rows - kh)
        k_nb = lax.dynamic_slice_in_dim(kg, r0, kh, axis=1)[:, :, col_idx]
        v_nb = lax.dynamic_slice_in_dim(vg, r0, kh, axis=1)[:, :, col_idx]
        dr = r0 + jnp.arange(kh, dtype=jnp.int32) - r + NA_KH - 1
        bias = jnp.take(rpb_cols, dr, axis=1).transpose(0, 2, 1, 3)
        s_loc = jnp.einsum('bqhd,bkqwhd->bhqkw', q_row, k_nb, preferred_element_type=jnp.float32) * scale
        s_loc = s_loc + bias.astype(jnp.float32)[None]
        s_ctx = jnp.einsum('bqhd,bchd->bhqc', q_row, k_ctx, preferred_element_type=jnp.float32) * scale
        s = jnp.concatenate([s_loc.reshape(B, H, GRID_W, n_loc), s_ctx], axis=-1)
        p = jax.nn.softmax(s, axis=-1).astype(v.dtype)
        p_loc = p[..., :n_loc].reshape(B, H, GRID_W, kh, NA_KW)
        return (jnp.einsum('bhqkw,bkqwhe->bqhe', p_loc, v_nb)
                + jnp.einsum('bhqc,bche->bqhe', p[..., n_loc:], v_ctx))

    out = lax.map(one_row, (jnp.arange(rows, dtype=jnp.int32), qg))
    return out.transpose(1, 0, 2, 3, 4).reshape(B, S, H, d)


def gated_merge(branches, gate_logits, w_branch, w_out):
    B, T, _ = gate_logits.shape
    gates = jax.nn.sigmoid(gate_logits.astype(jnp.float32)).astype(gate_logits.dtype)
    gates = gates.reshape(B, T, N_BRANCH, D_MODEL)
    merged = None
    for i, o in enumerate(branches):
        term = gates[:, :, i] * (o.reshape(B, T, BRANCH_W) @ w_branch[i])
        merged = term if merged is None else merged + term
    return merged @ w_out


def hybrid_mixer(n_ctx, n_lat, pos_r, pos_c, w_in, g_q_a, w_uq, g_kv_a, w_ukv, g_qn, g_kn,
                 rpb, w_branch, w_out, need_ctx):
    B, L, _ = n_ctx.shape
    T = L + n_lat.shape[1]
    h = jnp.concatenate([n_ctx, n_lat], axis=1)
    proj = h @ w_in
    (p_dq, p_dkv, p_kr, p_qb, p_kb, p_vb, p_qc, p_kc, p_vc, p_gate) = jnp.split(proj, IN_SPLITS, axis=-1)

    q_a = (rms_norm(p_dq, g_q_a) @ w_uq).reshape(B, T, MLA_HEADS, MLA_NOPE + MLA_ROPE)
    q_a = jnp.concatenate([q_a[..., :MLA_NOPE], axial_rope(q_a[..., MLA_NOPE:], pos_r, pos_c)], axis=-1)
    kv_a = (rms_norm(p_dkv, g_kv_a) @ w_ukv).reshape(B, T, MLA_HEADS, MLA_NOPE + MLA_V)
    k_pe = axial_rope(p_kr[:, :, None, :], pos_r, pos_c)
    k_a = jnp.concatenate([kv_a[..., :MLA_NOPE],
                           jnp.broadcast_to(k_pe, (B, T, MLA_HEADS, MLA_ROPE))], axis=-1)
    v_a = kv_a[..., MLA_NOPE:]

    q_b = axial_rope(rms_norm(p_qb.reshape(B, T, GQA_HEADS, HEAD_DIM), g_qn), pos_r, pos_c)
    k_b = axial_rope(rms_norm(p_kb.reshape(B, T, GQA_KV_HEADS, HEAD_DIM), g_kn), pos_r, pos_c)
    v_b = p_vb.reshape(B, T, GQA_KV_HEADS, HEAD_DIM)

    q_c = p_qc.reshape(B, T, NA_HEADS, HEAD_DIM)
    k_c = p_kc.reshape(B, T, NA_HEADS, HEAD_DIM)
    v_c = p_vc.reshape(B, T, NA_HEADS, HEAD_DIM)

    o_lat = (
        block_attention(q_a[:, L:], k_a, v_a, MLA_SCALE),
        block_attention(q_b[:, L:], k_b, v_b, HEAD_SCALE),
        neighbourhood_attention(q_c[:, L:], k_c[:, L:], v_c[:, L:], k_c[:, :L], v_c[:, :L], rpb, HEAD_SCALE),
    )
    y_lat = gated_merge(o_lat, p_gate[:, L:], w_branch, w_out)
    if not need_ctx:
        return None, y_lat
    o_ctx = (
        block_attention(q_a[:, :L], k_a[:, :L], v_a[:, :L], MLA_SCALE),
        block_attention(q_b[:, :L], k_b[:, :L], v_b[:, :L], HEAD_SCALE),
        block_attention(q_c[:, :L], k_c[:, :L], v_c[:, :L], HEAD_SCALE),
    )
    y_ctx = gated_merge(o_ctx, p_gate[:, :L], w_branch, w_out)
    return y_ctx, y_lat


def clamped_swiglu(hdn):
    x_glu = jnp.minimum(hdn[..., ::2], SWIGLU_LIMIT)
    x_lin = jnp.clip(hdn[..., 1::2], -SWIGLU_LIMIT, SWIGLU_LIMIT)
    return x_glu * jax.nn.sigmoid(SWIGLU_ALPHA * x_glu) * (x_lin + 1)


def moe_ffn(h, w_router, b_router, w1, b1, w2, b2):
    N, D = h.shape
    logits = (h @ w_router).astype(jnp.float32) + b_router.astype(jnp.float32)
    top_val, top_idx = lax.top_k(logits, TOP_K)
    top_w = jax.nn.softmax(top_val, axis=-1)
    A = N * TOP_K
    flat_e = top_idx.reshape(A)
    flat_t = jnp.repeat(jnp.arange(N, dtype=jnp.int32), TOP_K)
    flat_w = top_w.reshape(A)
    order = jnp.argsort(flat_e)
    se, st, sw = flat_e[order], flat_t[order], flat_w[order]
    counts = jnp.bincount(flat_e, length=N_EXPERTS)
    padded = (counts + MOE_BLOCK - 1) // MOE_BLOCK * MOE_BLOCK
    pend = jnp.cumsum(padded)
    dest = (pend - padded)[se] + jnp.arange(A, dtype=jnp.int32) - (jnp.cumsum(counts) - counts)[se]
    n_blocks = -(-A // MOE_BLOCK) + N_EXPERTS
    P = n_blocks * MOE_BLOCK
    buf_tok = jnp.full((P,), N, dtype=jnp.int32).at[dest].set(st)
    buf_w = jnp.zeros((P,), jnp.float32).at[dest].set(sw)
    starts = jnp.arange(n_blocks, dtype=jnp.int32) * MOE_BLOCK
    blk_e = jnp.minimum(jnp.sum(pend[None, :] <= starts[:, None], axis=1), N_EXPERTS - 1)
    h_pad = jnp.concatenate([h, jnp.zeros((1, D), h.dtype)], axis=0)

    def one_block(acc, blk):
        tok, wt, e = blk
        a = clamped_swiglu(h_pad[tok] @ w1[e] + b1[e])
        y = (a @ w2[e] + b2[e]).astype(jnp.float32) * wt[:, None]
        return acc.at[tok].add(y), None

    acc, _ = lax.scan(one_block, jnp.zeros((N + 1, D), jnp.float32),
                      (buf_tok.reshape(n_blocks, MOE_BLOCK), buf_w.reshape(n_blocks, MOE_BLOCK), blk_e))
    return acc[:N].astype(h.dtype)


def setup_inputs(seed: int = 0) -> dict:
    key = jax.random.key(seed)
    ks = jax.random.split(key, 25)
    f32 = jnp.float32
    D = D_MODEL
    L = DEPTH

    def nrm(k, shape, scale):
        return jax.random.normal(k, shape, f32) * scale

    def gain(k, shape):
        return 1.0 + 0.05 * jax.random.normal(k, shape, f32)

    return {
        'x': nrm(ks[0], (BATCH, SEQ, D), 1.0),
        'c': nrm(ks[1], (BATCH, D), 1.0),
        'ctx': nrm(ks[2], (BATCH, CTX_LEN, D), 1.0),
        'c_ctx': nrm(ks[3], (D,), 1.0),
        'w_mod': nrm(ks[4], (L, D, 6 * D), MOD_SCALE * D ** -0.5),
        'b_mod': nrm(ks[5], (L, 6 * D), 0.02),
        'g_mix': gain(ks[6], (L, D)),
        'w_in': nrm(ks[7], (L, D, IN_COLS), D ** -0.5),
        'g_q_a': gain(ks[8], (L, MLA_Q_RANK)),
        'w_uq': nrm(ks[9], (L, MLA_Q_RANK, MLA_HEADS * (MLA_NOPE + MLA_ROPE)), MLA_Q_RANK ** -0.5),
        'g_kv_a': gain(ks[10], (L, MLA_KV_RANK)),
        'w_ukv': nrm(ks[11], (L, MLA_KV_RANK, MLA_HEADS * (MLA_NOPE + MLA_V)), MLA_KV_RANK ** -0.5),
        'g_qn': gain(ks[12], (L, HEAD_DIM)),
        'g_kn': gain(ks[13], (L, HEAD_DIM)),
        'rpb': nrm(ks[14], (L, NA_HEADS, 2 * NA_KH - 1, 2 * NA_KW - 1), 0.05),
        'w_branch': nrm(ks[15], (L, N_BRANCH, BRANCH_W, D), BRANCH_W ** -0.5),
        'w_out': nrm(ks[16], (L, D, D), D ** -0.5),
        'g_ffn': gain(ks[17], (L, D)),
        'w_router': nrm(ks[18], (L, D, N_EXPERTS), D ** -0.5),
        'b_router': nrm(ks[19], (L, N_EXPERTS), 0.01),
        'w_exp1': nrm(ks[20], (L, N_EXPERTS, D, 2 * D_FF), D ** -0.5),
        'b_exp1': nrm(ks[21], (L, N_EXPERTS, 2 * D_FF), 0.02),
        'w_exp2': nrm(ks[22], (L, N_EXPERTS, D_FF, D), D_FF ** -0.5),
        'b_exp2': nrm(ks[23], (L, N_EXPERTS, D), 0.02),
        'g_final': gain(ks[24], (D,)),
    }


def reference(x, c, ctx, c_ctx, w_mod, b_mod, g_mix, w_in, g_q_a, w_uq, g_kv_a, w_ukv, g_qn, g_kn,
              rpb, w_branch, w_out, g_ffn, w_router, b_router, w_exp1, b_exp1, w_exp2, b_exp2, g_final):
    B, S, D = x.shape
    L = ctx.shape[1]
    t = jnp.arange(S, dtype=jnp.int32)
    zeros_ctx = jnp.zeros((L,), jnp.float32)
    pos_r = jnp.concatenate([zeros_ctx, (t // GRID_W).astype(jnp.float32)])
    pos_c = jnp.concatenate([zeros_ctx, (t % GRID_W).astype(jnp.float32)])
    x_lat, x_ctx = x, ctx
    for l in range(DEPTH):
        last = l == DEPTH - 1
        mod_lat = jax.nn.silu(c) @ w_mod[l] + b_mod[l]
        mod_ctx = jax.nn.silu(c_ctx) @ w_mod[l] + b_mod[l]
        sh1, sc1, gt1, sh2, sc2, gt2 = jnp.split(mod_lat[:, None, :], 6, axis=-1)
        csh1, csc1, cgt1, csh2, csc2, cgt2 = jnp.split(mod_ctx, 6, axis=-1)

        n_lat = modulate(rms_norm(x_lat, g_mix[l]), sh1, sc1)
        n_ctx = modulate(rms_norm(x_ctx, g_mix[l]), csh1, csc1)
        y_ctx, y_lat = hybrid_mixer(n_ctx, n_lat, pos_r, pos_c, w_in[l], g_q_a[l], w_uq[l], g_kv_a[l],
                                    w_ukv[l], g_qn[l], g_kn[l], rpb[l], w_branch[l], w_out[l],
                                    need_ctx=not last)
        x_lat = x_lat + gt1 * y_lat
        n_lat = modulate(rms_norm(x_lat, g_ffn[l]), sh2, sc2)
        if last:
            f_lat = moe_ffn(n_lat.reshape(B * S, D), w_router[l], b_router[l],
                            w_exp1[l], b_exp1[l], w_exp2[l], b_exp2[l]).reshape(B, S, D)
        else:
            x_ctx = x_ctx + cgt1 * y_ctx
            n_ctx = modulate(rms_norm(x_ctx, g_ffn[l]), csh2, csc2)
            f_all = moe_ffn(jnp.concatenate([n_ctx, n_lat], axis=1).reshape(B * (L + S), D),
                            w_router[l], b_router[l], w_exp1[l], b_exp1[l],
                            w_exp2[l], b_exp2[l]).reshape(B, L + S, D)
            x_ctx = x_ctx + cgt2 * f_all[:, :L]
            f_lat = f_all[:, L:]
        x_lat = x_lat + gt2 * f_lat
    return rms_norm(x_lat, g_final)
```

```python
import functools
from typing import NamedTuple

import numpy as np
import jax
import jax.numpy as jnp
from jax import lax
from jax.experimental import pallas as pl
from jax.experimental.pallas import tpu as pltpu

F32 = jnp.float32
BF16 = jnp.bfloat16

LANE = 128
VMEM_LIMIT = 56 * 1024 * 1024
NEG = -1e30


class Config(NamedTuple):
    grid_w: int = 64
    eps: float = 1e-6
    rope_theta: float = 10000.0
    mla_heads: int = 8
    mla_nope: int = 128
    mla_rope: int = 64
    mla_v: int = 128
    gqa_heads: int = 8
    gqa_kv_heads: int = 2
    head_dim: int = 128
    na_heads: int = 8
    top_k: int = 4
    swiglu_alpha: float = 1.702
    swiglu_limit: float = 7.0
    row_tile: int = 640
    kv_tile: int = 512
    moe_rows: int = 256
    gather_rows: int = 512
    combine_rows: int = 128


CFG = Config()


def _divisor_tile(n, target, mult):
    best = None
    for d in range(mult, min(n, target) + 1, mult):
        if n % d == 0:
            best = d
    assert best is not None, (n, target, mult)
    return best


def _params(*sem):
    return pltpu.CompilerParams(dimension_semantics=sem, vmem_limit_bytes=VMEM_LIMIT)


def _mod_kernel(c_ref, w_ref, b_ref, o_ref):
    c = c_ref[...]
    a = (c * jax.nn.sigmoid(c)).astype(BF16)
    o_ref[0] = jnp.dot(a, w_ref[0].astype(BF16), preferred_element_type=F32) + b_ref[0]


def _modulation(cc, w_mod, b_mod):
    depth, d, n6 = w_mod.shape
    tn = _divisor_tile(n6, 1024, LANE)
    return pl.pallas_call(
        _mod_kernel,
        out_shape=jax.ShapeDtypeStruct((depth, 8, n6), F32),
        grid=(depth, n6 // tn),
        in_specs=[pl.BlockSpec((8, d), lambda l, j: (0, 0)),
                  pl.BlockSpec((1, d, tn), lambda l, j: (l, 0, j)),
                  pl.BlockSpec((1, 1, tn), lambda l, j: (l, 0, j))],
        out_specs=pl.BlockSpec((1, 8, tn), lambda l, j: (l, 0, j)),
        compiler_params=_params("arbitrary", "arbitrary"),
        name="adaln_modulation",
    )(cc, w_mod, b_mod.reshape(depth, 1, n6))


def _ctx_mask(tile_idx, tiles_per_batch, tm, n_ctx):
    row = (tile_idx % tiles_per_batch) * tm + lax.broadcasted_iota(jnp.int32, (tm, 1), 0)
    return row < n_ctx


def _norm_kernel(*refs, has_res, has_router, tiles_per_batch, tm, n_ctx, eps, top_k):
    it = iter(refs)
    x_ref = next(it)
    f_ref = next(it) if has_res else None
    tab_ref = next(it)
    g_ref = next(it)
    if has_router:
        wr_ref = next(it)
        br_ref = next(it)
    xo_ref = next(it) if has_res else None
    h_ref = next(it)
    if has_router:
        ti_ref = next(it)
        tw_ref = next(it)

    is_ctx = _ctx_mask(pl.program_id(0), tiles_per_batch, tm, n_ctx)
    tab = tab_ref[0]

    def pick(k):
        return jnp.where(is_ctx, tab[2 * k:2 * k + 1], tab[2 * k + 1:2 * k + 2])

    x = x_ref[...]
    if has_res:
        x = x + pick(2) * f_ref[...]
        xo_ref[...] = x
    ms = jnp.mean(x * x, axis=-1, keepdims=True)
    y = x * lax.rsqrt(ms + eps) * g_ref[...]
    n = y * (1.0 + pick(1)) + pick(0)
    h_ref[...] = n.astype(BF16)

    if has_router:
        logits = jnp.dot(n, wr_ref[...], preferred_element_type=F32,
                         precision=lax.Precision.HIGHEST) + br_ref[...]
        lane = lax.broadcasted_iota(jnp.int32, logits.shape, 1)
        vals, idxs = [], []
        cur = logits
        for _ in range(top_k):
            m = jnp.max(cur, axis=-1, keepdims=True)
            idx = jnp.min(jnp.where(cur == m, lane, LANE), axis=-1, keepdims=True)
            vals.append(m)
            idxs.append(idx)
            cur = jnp.where(lane == idx, NEG, cur)
        es = [jnp.exp(v - vals[0]) for v in vals]
        den = es[0]
        for e in es[1:]:
            den = den + e
        ti = jnp.zeros(logits.shape, jnp.int32)
        tw = jnp.zeros(logits.shape, F32)
        for k in range(top_k):
            ti = jnp.where(lane == k, idxs[k], ti)
            tw = jnp.where(lane == k, es[k] / den, tw)
        ti_ref[...] = ti
        tw_ref[...] = tw


def _norm_mod(x, f, tab, g, router, *, t_rows, n_ctx, cfg):
    n_rows, d = x.shape
    tm = _divisor_tile(t_rows, 256, 8)
    tpb = t_rows // tm
    has_res = f is not None
    has_router = router is not None
    row = pl.BlockSpec((tm, d), lambda i: (i, 0))
    ins = [x] + ([f] if has_res else []) + [tab, g.reshape(1, d)]
    in_specs = [row] + ([row] if has_res else []) + [
        pl.BlockSpec((1, 8, d), lambda i: (i // tpb, 0, 0)),
        pl.BlockSpec((1, d), lambda i: (0, 0))]
    outs, out_specs = [], []
    if has_res:
        outs.append(jax.ShapeDtypeStruct((n_rows, d), F32))
        out_specs.append(row)
    outs.append(jax.ShapeDtypeStruct((n_rows, d), BF16))
    out_specs.append(row)
    if has_router:
        w_r, b_r = router
        n_exp = w_r.shape[1]
        assert n_exp <= LANE
        wr = jnp.zeros((d, LANE), F32).at[:, :n_exp].set(w_r)
        br = jnp.full((1, LANE), NEG, F32).at[0, :n_exp].set(b_r)
        ins += [wr, br]
        in_specs += [pl.BlockSpec((d, LANE), lambda i: (0, 0)), pl.BlockSpec((1, LANE), lambda i: (0, 0))]
        lane_row = pl.BlockSpec((tm, LANE), lambda i: (i, 0))
        outs += [jax.ShapeDtypeStruct((n_rows, LANE), jnp.int32), jax.ShapeDtypeStruct((n_rows, LANE), F32)]
        out_specs += [lane_row, lane_row]
    res = pl.pallas_call(
        functools.partial(_norm_kernel, has_res=has_res, has_router=has_router, tiles_per_batch=tpb,
                          tm=tm, n_ctx=n_ctx, eps=cfg.eps, top_k=cfg.top_k),
        out_shape=outs, grid=(n_rows // tm,), in_specs=in_specs, out_specs=out_specs,
        compiler_params=_params("arbitrary"),
        name="norm_modulate_router" if has_router else "norm_modulate",
    )(*ins)
    return list(res)


def _rope(t, cos, sin):
    return t * cos + pltpu.roll(t, LANE // 2, 1) * sin


def _mm_kernel(*refs, epilogue, n_extra):
    a_ref, w_ref = refs[0], refs[1]
    extra = refs[2:2 + n_extra]
    outs = refs[2 + n_extra:]
    acc = jnp.dot(a_ref[...], w_ref[...], preferred_element_type=F32)
    epilogue(acc, extra, outs)


def _matmul(a, w, epilogue, extras, out_shape, out_spec, *, tm, tn, a_col=0, name):
    m = a.shape[0]
    k, nw = w.shape
    in_specs = [pl.BlockSpec((tm, k), lambda i, j: (i, a_col)), pl.BlockSpec((k, tn), lambda i, j: (0, j))]
    in_specs += [s for _, s in extras]
    return pl.pallas_call(
        functools.partial(_mm_kernel, epilogue=epilogue, n_extra=len(extras)),
        out_shape=out_shape, grid=(m // tm, nw // tn), in_specs=in_specs, out_specs=out_spec,
        compiler_params=_params("arbitrary", "arbitrary"), name=name,
    )(a, w, *[x for x, _ in extras])


def _epi_rmsnorm(acc, extra, outs, *, eps):
    (g_ref,), (o_ref,) = extra, outs
    ms = jnp.mean(acc * acc, axis=-1, keepdims=True)
    o_ref[...] = (acc * lax.rsqrt(ms + eps) * g_ref[...]).astype(o_ref.dtype)


def _epi_scale(acc, extra, outs):
    (s_ref,), (o_ref,) = extra, outs
    o_ref[...] = (acc * s_ref[...]).astype(o_ref.dtype)


def _epi_plain(acc, extra, outs):
    outs[0][...] = acc.astype(outs[0].dtype)


def _epi_rope_tiles(acc, extra, outs, *, rope_tiles, norm, eps, scale):
    if norm:
        g_ref, cos_ref, sin_ref = extra
    else:
        cos_ref, sin_ref = extra
    o_ref = outs[0]
    cos, sin = cos_ref[...], sin_ref[...]
    for t in range(acc.shape[1] // LANE):
        v = acc[:, t * LANE:(t + 1) * LANE]
        if norm:
            ms = jnp.mean(v * v, axis=-1, keepdims=True)
            v = v * lax.rsqrt(ms + eps) * g_ref[...]
        if rope_tiles[t % len(rope_tiles)]:
            v = _rope(v, cos, sin)
        o_ref[:, t * LANE:(t + 1) * LANE] = (v * scale).astype(o_ref.dtype)


def _epi_mla_key(acc, extra, outs):
    (kpe_ref,), (o_ref,) = extra, outs
    o_ref[:, :LANE] = acc.astype(o_ref.dtype)
    o_ref[:, LANE:] = kpe_ref[...]


def _epi_residual(acc, extra, outs, *, tiles_per_batch, tm, n_ctx):
    (x_ref, tab_ref), (o_ref,) = extra, outs
    is_ctx = _ctx_mask(pl.program_id(0), tiles_per_batch, tm, n_ctx)
    tab = tab_ref[0]
    gate = jnp.where(is_ctx, tab[0:1], tab[1:2])
    o_ref[...] = x_ref[...] + gate * acc


def _attn_kernel(q_ref, k_ref, v_ref, o_ref, *, n_ctx, n_steps, tk):
    q = q_ref[...]
    tq = q.shape[0]
    dv = v_ref.shape[1]

    def step(start, size, carry):
        m, l, acc = carry
        k = k_ref[pl.ds(start, size), :]
        v = v_ref[pl.ds(start, size), :]
        s = lax.dot_general(q, k, (((1,), (1,)), ((), ())), preferred_element_type=F32)
        m_new = jnp.maximum(m, jnp.max(s, axis=-1, keepdims=True))
        alpha = jnp.exp(m - m_new)
        p = jnp.exp(s - m_new)
        l = alpha * l + jnp.sum(p, axis=-1, keepdims=True)
        acc = alpha * acc + jnp.dot(p.astype(BF16), v, preferred_element_type=F32)
        return m_new, l, acc

    carry = (jnp.full((tq, 1), NEG, F32), jnp.zeros((tq, 1), F32), jnp.zeros((tq, dv), F32))
    carry = step(0, n_ctx, carry)

    def body(c, carry):
        return step(pl.multiple_of(n_ctx + c * tk, n_ctx), tk, carry)

    _, l, acc = lax.fori_loop(0, n_steps, body, carry)
    o_ref[...] = (acc / l).astype(o_ref.dtype)


def _attention(q, k, v, out, *, batch, n_ctx, n_lat, heads, kv_heads, dk, dv, q_col, k_col, v_col,
               q_tiles, cfg, name):
    t_rows = n_ctx + n_lat
    tq = n_ctx
    tpb = t_rows // tq
    q_off = 1 if q_tiles == "lat" else 0
    n_q = tpb - 1 if q_tiles == "lat" else 1
    group = heads // kv_heads
    tk = _divisor_tile(n_lat, cfg.kv_tile, tq)
    n_steps = n_lat // tk if q_tiles == "lat" else 0
    k_rows = t_rows if q_tiles == "lat" else n_ctx
    kpb = t_rows // k_rows

    def kern(q_ref, k_ref, v_ref, _, o_ref):
        _attn_kernel(q_ref, k_ref, v_ref, o_ref, n_ctx=n_ctx, n_steps=n_steps, tk=tk)

    return pl.pallas_call(
        kern, out_shape=jax.ShapeDtypeStruct(out.shape, out.dtype),
        grid=(batch, heads, n_q),
        in_specs=[pl.BlockSpec((tq, dk), lambda b, h, i: (b * tpb + i + q_off, q_col + h)),
                  pl.BlockSpec((k_rows, dk), lambda b, h, i: (b * kpb, k_col + h // group)),
                  pl.BlockSpec((k_rows, dv), lambda b, h, i: (b * kpb, v_col + h // group)),
                  pl.BlockSpec(memory_space=pl.ANY)],
        out_specs=pl.BlockSpec((tq, dv), lambda b, h, i: (b * tpb + i + q_off, h)),
        input_output_aliases={3: 0},
        compiler_params=_params("arbitrary", "arbitrary", "arbitrary"), name=name,
    )(q, k, v, out)


def _natten_kernel(q_ref, k_ref, v_ref, bias_ref, o_ref, *, n_ctx, grid_w, rows, kh):
    r = pl.program_id(2)
    r0 = jnp.clip(r - kh // 2, 0, rows - kh)
    start = pl.multiple_of(n_ctx + r0 * grid_w, grid_w)
    q = q_ref[...]
    dn = (((1,), (1,)), ((), ()))
    k_loc = k_ref[pl.ds(start, kh * grid_w), :]
    v_loc = v_ref[pl.ds(start, kh * grid_w), :]
    s_loc = lax.dot_general(q, k_loc, dn, preferred_element_type=F32) + bias_ref[0, 0]
    s_ctx = lax.dot_general(q, k_ref[pl.ds(0, n_ctx), :], dn, preferred_element_type=F32)
    m = jnp.maximum(jnp.max(s_loc, axis=-1, keepdims=True), jnp.max(s_ctx, axis=-1, keepdims=True))
    p_loc = jnp.exp(s_loc - m)
    p_ctx = jnp.exp(s_ctx - m)
    l = jnp.sum(p_loc, axis=-1, keepdims=True) + jnp.sum(p_ctx, axis=-1, keepdims=True)
    acc = jnp.dot(p_loc.astype(BF16), v_loc, preferred_element_type=F32)
    acc = acc + jnp.dot(p_ctx.astype(BF16), v_ref[pl.ds(0, n_ctx), :], preferred_element_type=F32)
    o_ref[...] = (acc / l).astype(o_ref.dtype)


def _natten_bias(rpb, grid_w, kh, kw):
    heads = rpb.shape[0]
    full_kh = (rpb.shape[1] + 1) // 2
    col = np.arange(grid_w)
    col0 = np.clip(col - kw // 2, 0, grid_w - kw)
    kc = np.arange(grid_w)
    inside = (kc[None, :] >= col0[:, None]) & (kc[None, :] < col0[:, None] + kw)
    dc = np.clip(kc[None, :] - col[:, None] + kw - 1, 0, 2 * kw - 2)
    d = np.arange(kh)
    i = np.arange(kh)
    dr = np.clip(i[None, :] - d[:, None] + full_kh - 1, 0, 2 * full_kh - 2)
    b = rpb[:, dr[:, None, :, None], dc[None, :, None, :]]
    b = jnp.where(inside[None, None, :, None, :], b, NEG)
    return b.reshape(heads, kh, grid_w, kh * grid_w).astype(F32)


def _natten(qkv, out, bias, *, batch, n_ctx, n_lat, heads, hd, grid_w, kh, name):
    t_rows = n_ctx + n_lat
    rows = n_lat // grid_w
    assert n_ctx % grid_w == 0 and t_rows % grid_w == 0
    rpb_t = t_rows // grid_w
    off = n_ctx // grid_w
    o_spec = pl.BlockSpec((grid_w, hd), lambda b, h, r: (b * rpb_t + off + r, h))

    def bias_idx(b, h, r):
        return (h, r - jnp.clip(r - kh // 2, 0, rows - kh), 0, 0)

    def kern(q_ref, k_ref, v_ref, bias_ref, _, o_ref):
        _natten_kernel(q_ref, k_ref, v_ref, bias_ref, o_ref, n_ctx=n_ctx, grid_w=grid_w, rows=rows, kh=kh)

    return pl.pallas_call(
        kern, out_shape=jax.ShapeDtypeStruct(out.shape, out.dtype),
        grid=(batch, heads, rows),
        in_specs=[pl.BlockSpec((grid_w, hd), lambda b, h, r: (b * rpb_t + off + r, h)),
                  pl.BlockSpec((t_rows, hd), lambda b, h, r: (b, heads + h)),
                  pl.BlockSpec((t_rows, hd), lambda b, h, r: (b, 2 * heads + h)),
                  pl.BlockSpec((1, 1, grid_w, kh * grid_w), bias_idx),
                  pl.BlockSpec(memory_space=pl.ANY)],
        out_specs=o_spec, input_output_aliases={4: 0},
        compiler_params=_params("arbitrary", "arbitrary", "arbitrary"), name=name,
    )(qkv, qkv, qkv, bias, out)


def _merge_kernel(h_ref, oa_ref, ob_ref, oc_ref, wg_ref, wb_ref, o_ref):
    h = h_ref[...]
    acc = None
    for i, o in enumerate((oa_ref, ob_ref, oc_ref)):
        gate = jax.nn.sigmoid(jnp.dot(h, wg_ref[i], preferred_element_type=F32))
        term = gate * jnp.dot(o[...], wb_ref[i], preferred_element_type=F32)
        acc = term if acc is None else acc + term
    o_ref[...] = acc.astype(o_ref.dtype)


def _merge(h, o_a, o_b, o_c, wg, wb, *, tm, tn):
    n_rows, d = h.shape
    bw = o_a.shape[1]
    nb = wg.shape[0]
    return pl.pallas_call(
        _merge_kernel, out_shape=jax.ShapeDtypeStruct((n_rows, d), BF16),
        grid=(n_rows // tm, d // tn),
        in_specs=[pl.BlockSpec((tm, d), lambda i, j: (i, 0))]
        + [pl.BlockSpec((tm, bw), lambda i, j: (i, 0))] * 3
        + [pl.BlockSpec((nb, d, tn), lambda i, j: (0, 0, j)),
           pl.BlockSpec((nb, bw, tn), lambda i, j: (0, 0, j))],
        out_specs=pl.BlockSpec((tm, tn), lambda i, j: (i, j)),
        compiler_params=_params("arbitrary", "arbitrary"), name="gated_merge",
    )(h, o_a, o_b, o_c, wg, wb)


def _dispatch_kernel(tok_ref, src_ref, dst_ref, sem, *, rows):
    base = pl.program_id(0) * rows

    def issue(r, _):
        pltpu.make_async_copy(src_ref.at[tok_ref[base + r]], dst_ref.at[base + r], sem).start()
        return 0

    lax.fori_loop(0, rows, issue, 0)

    def drain(r, _):
        pltpu.make_async_copy(src_ref.at[0], dst_ref.at[base + r], sem).wait()
        return 0

    lax.fori_loop(0, rows, drain, 0)


def _dispatch(row_tok, h3, n_out, rows):
    _, s, lanes = h3.shape
    return pl.pallas_call(
        functools.partial(_dispatch_kernel, rows=rows),
        out_shape=jax.ShapeDtypeStruct((n_out, s, lanes), h3.dtype),
        grid_spec=pltpu.PrefetchScalarGridSpec(
            num_scalar_prefetch=1, grid=(n_out // rows,),
            in_specs=[pl.BlockSpec(memory_space=pl.ANY)],
            out_specs=pl.BlockSpec(memory_space=pl.ANY),
            scratch_shapes=[pltpu.SemaphoreType.DMA]),
        compiler_params=pltpu.CompilerParams(dimension_semantics=("arbitrary",),
                                             disable_bounds_checks=True),
        name="moe_dispatch",
    )(row_tok, h3)


def _swiglu(x_glu, x_lin, alpha, limit):
    x_glu = jnp.minimum(x_glu, limit)
    x_lin = jnp.clip(x_lin, -limit, limit)
    return x_glu * jax.nn.sigmoid(alpha * x_glu) * (x_lin + 1.0)


def _moe_up_kernel(be_ref, nu_ref, x_ref, wg_ref, wl_ref, bg_ref, bl_ref, a_ref, *, tc, alpha, limit):
    i = pl.program_id(0)

    @pl.when(i < nu_ref[0])
    def _():
        x = x_ref[...]
        for c in range(a_ref.shape[1] // tc):
            sl = slice(c * tc, (c + 1) * tc)
            hg = jnp.dot(x, wg_ref[0, :, sl], preferred_element_type=F32) + bg_ref[0, :, sl]
            hl = jnp.dot(x, wl_ref[0, :, sl], preferred_element_type=F32) + bl_ref[0, :, sl]
            a_ref[:, sl] = _swiglu(hg, hl, alpha, limit).astype(a_ref.dtype)

    @pl.when(i >= nu_ref[0])
    def _():
        a_ref[...] = jnp.zeros(a_ref.shape, a_ref.dtype)


def _moe_down_kernel(be_ref, nu_ref, a_ref, w_ref, b_ref, wt_ref, y_ref):
    i = pl.program_id(0)

    @pl.when(i < nu_ref[0])
    def _():
        y = jnp.dot(a_ref[...], w_ref[0], preferred_element_type=F32) + b_ref[0]
        y_ref[...] = y * wt_ref[...]

    @pl.when(i >= nu_ref[0])
    def _():
        y_ref[...] = jnp.zeros(y_ref.shape, y_ref.dtype)


def _moe_experts(blk_e, n_used, xg, w1g, w1l, b1g, b1l, w2, b2, row_w, *, bm, cfg):
    p_rows, d = xg.shape
    n_exp, _, ff = w1g.shape
    nb = p_rows // bm
    tc = _divisor_tile(ff, 512, LANE)
    a = pl.pallas_call(
        functools.partial(_moe_up_kernel, tc=tc, alpha=cfg.swiglu_alpha, limit=cfg.swiglu_limit),
        out_shape=jax.ShapeDtypeStruct((p_rows, ff), BF16),
        grid_spec=pltpu.PrefetchScalarGridSpec(
            num_scalar_prefetch=2, grid=(nb,),
            in_specs=[pl.BlockSpec((bm, d), lambda i, be, nu: (i, 0)),
                      pl.BlockSpec((1, d, ff), lambda i, be, nu: (be[i], 0, 0)),
                      pl.BlockSpec((1, d, ff), lambda i, be, nu: (be[i], 0, 0)),
                      pl.BlockSpec((1, 1, ff), lambda i, be, nu: (be[i], 0, 0)),
                      pl.BlockSpec((1, 1, ff), lambda i, be, nu: (be[i], 0, 0))],
            out_specs=pl.BlockSpec((bm, ff), lambda i, be, nu: (i, 0))),
        compiler_params=_params("arbitrary"), name="moe_up_swiglu",
    )(blk_e, n_used, xg, w1g, w1l, b1g, b1l)
    return pl.pallas_call(
        _moe_down_kernel,
        out_shape=jax.ShapeDtypeStruct((p_rows, d), F32),
        grid_spec=pltpu.PrefetchScalarGridSpec(
            num_scalar_prefetch=2, grid=(nb,),
            in_specs=[pl.BlockSpec((bm, ff), lambda i, be, nu: (i, 0)),
                      pl.BlockSpec((1, ff, d), lambda i, be, nu: (be[i], 0, 0)),
                      pl.BlockSpec((1, 1, d), lambda i, be, nu: (be[i], 0, 0)),
                      pl.BlockSpec((bm, 1), lambda i, be, nu: (i, 0))],
            out_specs=pl.BlockSpec((bm, d), lambda i, be, nu: (i, 0))),
        compiler_params=_params("arbitrary"), name="moe_down",
    )(blk_e, n_used, a, w2, b2, row_w)


def _combine_kernel(dest_ref, y_ref, o_ref, buf, sem, *, tm, top_k):
    base = pl.program_id(0) * tm

    def issue(t, _):
        for k in range(top_k):
            pltpu.make_async_copy(y_ref.at[dest_ref[(base + t) * top_k + k]], buf.at[k, t], sem).start()
        return 0

    lax.fori_loop(0, tm, issue, 0)

    def drain(t, _):
        for k in range(top_k):
            pltpu.make_async_copy(y_ref.at[0], buf.at[k, t], sem).wait()
        return 0

    lax.fori_loop(0, tm, drain, 0)
    acc = buf[0]
    for k in range(1, top_k):
        acc = acc + buf[k]
    o_ref[...] = acc


def _combine(dest, y3, n_tok, *, tm, top_k):
    _, s, lanes = y3.shape
    return pl.pallas_call(
        functools.partial(_combine_kernel, tm=tm, top_k=top_k),
        out_shape=jax.ShapeDtypeStruct((n_tok, s, lanes), F32),
        grid_spec=pltpu.PrefetchScalarGridSpec(
            num_scalar_prefetch=1, grid=(n_tok // tm,),
            in_specs=[pl.BlockSpec(memory_space=pl.ANY)],
            out_specs=pl.BlockSpec((tm, s, lanes), lambda i, dest: (i, 0, 0)),
            scratch_shapes=[pltpu.VMEM((top_k, tm, s, lanes), F32), pltpu.SemaphoreType.DMA]),
        compiler_params=pltpu.CompilerParams(dimension_semantics=("arbitrary",),
                                             vmem_limit_bytes=VMEM_LIMIT, disable_bounds_checks=True),
        name="moe_combine",
    )(dest, y3)


def _moe(h, top_i, top_w, valid, w1g, w1l, b1g, b1l, w2, b2, *, cfg):
    n_tok, d = h.shape
    n_exp = w1g.shape[0]
    kk = cfg.top_k
    bm = cfg.moe_rows
    idx = top_i[:, :kk]
    wts = top_w[:, :kk]
    sel = (idx[:, :, None] == jnp.arange(n_exp, dtype=jnp.int32)[None, None, :]) & valid[:, None, None]
    sel = jnp.any(sel, axis=1).astype(jnp.int32)
    counts = jnp.sum(sel, axis=0)
    rank = jnp.cumsum(sel, axis=0) - sel
    padded = (counts + bm - 1) // bm * bm
    gend = jnp.cumsum(padded)
    gstart = gend - padded
    dest = gstart[idx] + jnp.take_along_axis(rank, idx, axis=1)
    n_assign = n_tok * kk
    n_blocks = -(-n_assign // bm) + n_exp
    p_rows = n_blocks * bm
    gr = _divisor_tile(p_rows, cfg.gather_rows, bm)
    dest_w = jnp.where(valid[:, None], dest, p_rows).reshape(-1)
    tok = jnp.repeat(jnp.arange(n_tok, dtype=jnp.int32), kk)
    row_tok = jnp.zeros((p_rows,), jnp.int32).at[dest_w].set(tok, mode="drop")
    row_w = jnp.zeros((p_rows,), F32).at[dest_w].set(wts.reshape(-1), mode="drop")
    starts = jnp.arange(n_blocks, dtype=jnp.int32) * bm
    blk_e = jnp.minimum(jnp.sum(gend[None, :] <= starts[:, None], axis=1), n_exp - 1).astype(jnp.int32)
    n_used = (gend[-1] // bm).astype(jnp.int32).reshape(1)

    xg = _dispatch(row_tok, h.reshape(n_tok, d // LANE, LANE), p_rows, gr).reshape(p_rows, d)
    y = _moe_experts(blk_e, n_used, xg, w1g, w1l, b1g, b1l, w2, b2, row_w.reshape(p_rows, 1), bm=bm, cfg=cfg)
    dest_r = jnp.where(valid[:, None], dest, 0).astype(jnp.int32).reshape(-1)
    tmc = _divisor_tile(n_tok, cfg.combine_rows, 8)
    f = _combine(dest_r, y.reshape(p_rows, d // LANE, LANE), n_tok, tm=tmc, top_k=kk)
    return f.reshape(n_tok, d)


def _final_kernel(x_ref, f_ref, gate_ref, g_ref, o_ref, *, eps):
    x = x_ref[...] + gate_ref[0] * f_ref[...]
    ms = jnp.mean(x * x, axis=-1, keepdims=True)
    o_ref[0] = x * lax.rsqrt(ms + eps) * g_ref[...]


def _final_norm(x, f, gate, g, *, batch, n_ctx, n_lat, eps):
    d = x.shape[1]
    ts = n_ctx
    tpb = (n_ctx + n_lat) // ts
    row = pl.BlockSpec((ts, d), lambda b, s: (b * tpb + 1 + s, 0))
    return pl.pallas_call(
        functools.partial(_final_kernel, eps=eps),
        out_shape=jax.ShapeDtypeStruct((batch, n_lat, d), F32),
        grid=(batch, n_lat // ts),
        in_specs=[row, row, pl.BlockSpec((1, 1, d), lambda b, s: (b, 0, 0)),
                  pl.BlockSpec((1, d), lambda b, s: (0, 0))],
        out_specs=pl.BlockSpec((1, ts, d), lambda b, s: (b, s, 0)),
        compiler_params=_params("arbitrary", "arbitrary"), name="final_norm",
    )(x, f, gate, g.reshape(1, d))


def _rope_tables(kind, n_ctx, n_lat, grid_w, theta, dims):
    half = dims // 2
    nf = half // 2
    freqs = theta ** (-np.arange(0, half, 2, dtype=np.float64) / half)
    t = np.arange(n_lat)
    pos_r = np.concatenate([np.zeros(n_ctx), t // grid_w]).astype(np.float64)
    pos_c = np.concatenate([np.zeros(n_ctx), t % grid_w]).astype(np.float64)
    ang_r = pos_r[:, None] * freqs[None, :]
    ang_c = pos_c[:, None] * freqs[None, :]
    ang = np.concatenate([ang_r, ang_c], axis=1)
    rows = n_ctx + n_lat
    cos = np.ones((rows, LANE), np.float32)
    sin = np.zeros((rows, LANE), np.float32)
    w = 2 * nf
    cos[:, :w] = np.cos(ang)
    cos[:, LANE // 2:LANE // 2 + w] = np.cos(ang)
    sin[:, :w] = -np.sin(ang)
    sin[:, LANE // 2:LANE // 2 + w] = np.sin(ang)
    del kind
    return jnp.asarray(cos), jnp.asarray(sin)


def _axial_lane_src(dims):
    half = dims // 2
    nf = half // 2
    src = -np.ones((LANE,), np.int64)
    src[0:nf] = np.arange(nf)
    src[nf:2 * nf] = half + np.arange(nf)
    src[LANE // 2:LANE // 2 + nf] = nf + np.arange(nf)
    src[LANE // 2 + nf:LANE // 2 + 2 * nf] = half + nf + np.arange(nf)
    return src


def _take_cols(w, src):
    src = np.asarray(src)
    out = jnp.take(w, jnp.asarray(np.maximum(src, 0)), axis=-1)
    return jnp.where(jnp.asarray(src >= 0), out, jnp.zeros((), w.dtype))


def _forward(x, c, ctx, c_ctx, w_mod, b_mod, g_mix, w_in, g_q_a, w_uq, g_kv_a, w_ukv, g_qn, g_kn,
             rpb, w_branch, w_out, g_ffn, w_router, b_router, w_exp1, b_exp1, w_exp2, b_exp2, g_final,
             cfg):
    batch, n_lat, d = x.shape
    n_ctx = ctx.shape[1]
    depth = w_mod.shape[0]
    t_rows = n_ctx + n_lat
    n_rows = batch * t_rows
    hd = cfg.head_dim
    q_rank, kv_rank = g_q_a.shape[1], g_kv_a.shape[1]
    ha, hb, hkv, hc = cfg.mla_heads, cfg.gqa_heads, cfg.gqa_kv_heads, cfg.na_heads
    nope, rdim, vdim = cfg.mla_nope, cfg.mla_rope, cfg.mla_v
    assert nope == LANE and vdim == LANE and hd == LANE and rdim <= LANE // 2
    assert n_lat % n_ctx == 0 and n_ctx % 16 == 0 and batch <= 7
    kh_full = (rpb.shape[2] + 1) // 2
    kw = (rpb.shape[3] + 1) // 2
    rows = n_lat // cfg.grid_w
    kh = min(kh_full, rows)
    n_branch = w_branch.shape[1]
    assert n_branch == 3

    tm = _divisor_tile(t_rows, cfg.row_tile, 16)
    tpb = t_rows // tm

    sizes = (q_rank, kv_rank, rdim, hb * hd, hkv * hd, hkv * hd, hc * hd, hc * hd, hc * hd, n_branch * d)
    offs = np.concatenate([[0], np.cumsum(sizes)])

    mla_src = _axial_lane_src(rdim)
    gqa_src = _axial_lane_src(hd)
    cos_a, sin_a = _rope_tables("mla", n_ctx, n_lat, cfg.grid_w, cfg.rope_theta, rdim)
    cos_b, sin_b = _rope_tables("gqa", n_ctx, n_lat, cfg.grid_w, cfg.rope_theta, hd)
    rope_spec = pl.BlockSpec((tm, LANE), lambda i, j: (i % tpb, 0))
    mla_scale = float((nope + rdim) ** -0.5)
    head_scale = float(hd ** -0.5)

    cc = jnp.zeros((8, d), F32).at[:batch].set(c).at[batch].set(c_ctx)
    mod = _modulation(cc, w_mod, b_mod)

    def table(layer, chunks):
        rows_ = []
        for ch in chunks:
            sl = mod[layer, :, ch * d:(ch + 1) * d]
            rows_.append(jnp.broadcast_to(sl[batch][None], (batch, d)))
            rows_.append(sl[:batch])
        tab = jnp.stack(rows_, axis=1)
        return jnp.pad(tab, ((0, 0), (0, 8 - tab.shape[1]), (0, 0)))

    xa = jnp.concatenate([ctx, x], axis=1).reshape(n_rows, d)
    pend_f = None
    pend_tab = None
    is_lat = (jnp.arange(n_rows, dtype=jnp.int32) % t_rows) >= n_ctx

    for layer in range(depth):
        last = layer == depth - 1
        wi = w_in[layer]

        w_d = wi[:, offs[0]:offs[2]].astype(BF16)
        g_d = jnp.concatenate([g_q_a[layer], g_kv_a[layer]]).reshape(1, -1)
        w_kr = _take_cols(wi[:, offs[2]:offs[3]], mla_src).astype(BF16)
        uq = w_uq[layer].reshape(q_rank, ha, nope + rdim)
        uq = jnp.concatenate([uq[:, :, :nope], _take_cols(uq[:, :, nope:], mla_src)], axis=-1)
        uq = uq.reshape(q_rank, ha * 2 * LANE).astype(BF16)
        ukv = w_ukv[layer].reshape(kv_rank, ha, nope + vdim)
        uk = ukv[:, :, :nope].reshape(kv_rank, ha * nope).astype(BF16)
        uv = ukv[:, :, nope:].reshape(kv_rank, ha * vdim).astype(BF16)
        gq_perm = (np.arange(hb)[:, None] * hd + gqa_src[None, :]).reshape(-1)
        gk_perm = (np.arange(hkv)[:, None] * hd + gqa_src[None, :]).reshape(-1)
        w_qb = _take_cols(wi[:, offs[3]:offs[4]], gq_perm).astype(BF16)
        w_kb = _take_cols(wi[:, offs[4]:offs[5]], gk_perm).astype(BF16)
        w_vb = wi[:, offs[5]:offs[6]].astype(BF16)
        g_qn_p = _take_cols(g_qn[layer], gqa_src).reshape(1, hd)
        g_kn_p = _take_cols(g_kn[layer], gqa_src).reshape(1, hd)
        w_c = wi[:, offs[6]:offs[9]].astype(BF16)
        c_scale = jnp.concatenate([jnp.full((hc * hd,), head_scale, F32),
                                   jnp.ones((2 * hc * hd,), F32)]).reshape(1, -1)
        w_g = wi[:, offs[9]:offs[10]].reshape(d, n_branch, d).transpose(1, 0, 2).astype(BF16)
        w_b = w_branch[layer].astype(BF16)
        w_o = w_out[layer].astype(BF16)
        ff = w_exp2.shape[2]
        w1g = w_exp1[layer, :, :, 0::2].astype(BF16)
        w1l = w_exp1[layer, :, :, 1::2].astype(BF16)
        b1g = b_exp1[layer, :, 0::2].reshape(-1, 1, ff)
        b1l = b_exp1[layer, :, 1::2].reshape(-1, 1, ff)
        w2 = w_exp2[layer].astype(BF16)
        b2 = b_exp2[layer].reshape(-1, 1, d)

        tab1 = table(layer, (0, 1)) if pend_f is None else jnp.concatenate(
            [table(layer, (0, 1))[:, :4], pend_tab[:, :2], jnp.zeros((batch, 2, d), F32)], axis=1)
        res = _norm_mod(xa, pend_f, tab1, g_mix[layer], None, t_rows=t_rows, n_ctx=n_ctx, cfg=cfg)
        if pend_f is not None:
            xa, h = res
        else:
            (h,) = res

        dn = _matmul(h, w_d, functools.partial(_epi_rmsnorm, eps=cfg.eps),
                     [(g_d, pl.BlockSpec((1, q_rank), lambda i, j: (0, j)))],
                     jax.ShapeDtypeStruct((n_rows, q_rank + kv_rank), BF16),
                     pl.BlockSpec((tm, q_rank), lambda i, j: (i, j)), tm=tm, tn=q_rank, name="mla_down")
        assert q_rank == kv_rank
        k_pe = _matmul(h, w_kr,
                       functools.partial(_epi_rope_tiles, rope_tiles=(True,), norm=False, eps=cfg.eps, scale=1.0),
                       [(cos_a, rope_spec), (sin_a, rope_spec)],
                       jax.ShapeDtypeStruct((n_rows, LANE), BF16),
                       pl.BlockSpec((tm, LANE), lambda i, j: (i, 0)), tm=tm, tn=LANE, name="mla_rope_key")
        q_a = _matmul(dn, uq,
                      functools.partial(_epi_rope_tiles, rope_tiles=(False, True), norm=False, eps=cfg.eps,
                                        scale=mla_scale),
                      [(cos_a, rope_spec), (sin_a, rope_spec)],
                      jax.ShapeDtypeStruct((n_rows, ha * 2 * LANE), BF16),
                      pl.BlockSpec((tm, 2 * LANE), lambda i, j: (i, j)), tm=tm, tn=2 * LANE, a_col=0,
                      name="mla_q_up")
        k_a = _matmul(dn, uk, _epi_mla_key,
                      [(k_pe, pl.BlockSpec((tm, LANE), lambda i, j: (i, 0)))],
                      jax.ShapeDtypeStruct((n_rows, ha * 2 * LANE), BF16),
                      pl.BlockSpec((tm, 2 * LANE), lambda i, j: (i, j)), tm=tm, tn=LANE, a_col=1,
                      name="mla_k_up")
        tn_v = _divisor_tile(ha * vdim, 4 * LANE, LANE)
        v_a = _matmul(dn, uv, _epi_plain, [],
                      jax.ShapeDtypeStruct((n_rows, ha * vdim), BF16),
                      pl.BlockSpec((tm, tn_v), lambda i, j: (i, j)), tm=tm, tn=tn_v, a_col=1, name="mla_v_up")

        tn_q = _divisor_tile(hb * hd, 4 * LANE, LANE)
        q_b = _matmul(h, w_qb,
                      functools.partial(_epi_rope_tiles, rope_tiles=(True,), norm=True, eps=cfg.eps,
                                        scale=head_scale),
                      [(g_qn_p, pl.BlockSpec((1, hd), lambda i, j: (0, 0))), (cos_b, rope_spec), (sin_b, rope_spec)],
                      jax.ShapeDtypeStruct((n_rows, hb * hd), BF16),
                      pl.BlockSpec((tm, tn_q), lambda i, j: (i, j)), tm=tm, tn=tn_q, name="gqa_q")
        tn_k = _divisor_tile(hkv * hd, 4 * LANE, LANE)
        k_b = _matmul(h, w_kb,
                      functools.partial(_epi_rope_tiles, rope_tiles=(True,), norm=True, eps=cfg.eps, scale=1.0),
                      [(g_kn_p, pl.BlockSpec((1, hd), lambda i, j: (0, 0))), (cos_b, rope_spec), (sin_b, rope_spec)],
                      jax.ShapeDtypeStruct((n_rows, hkv * hd), BF16),
                      pl.BlockSpec((tm, tn_k), lambda i, j: (i, j)), tm=tm, tn=tn_k, name="gqa_k")
        v_b = _matmul(h, w_vb, _epi_plain, [],
                      jax.ShapeDtypeStruct((n_rows, hkv * hd), BF16),
                      pl.BlockSpec((tm, tn_k), lambda i, j: (i, j)), tm=tm, tn=tn_k, name="gqa_v")

        tn_c = _divisor_tile(3 * hc * hd, 4 * LANE, LANE)
        qkv_c = _matmul(h, w_c, _epi_scale, [(c_scale, pl.BlockSpec((1, tn_c), lambda i, j: (0, j)))],
                        jax.ShapeDtypeStruct((n_rows, 3 * hc * hd), BF16),
                        pl.BlockSpec((tm, tn_c), lambda i, j: (i, j)), tm=tm, tn=tn_c, name="natten_qkv")

        common = dict(batch=batch, n_ctx=n_ctx, n_lat=n_lat, cfg=cfg)
        assert ha * vdim == hb * hd == hc * hd
        o_a = o_b = o_c = jnp.zeros((n_rows, hc * hd), BF16)
        spec_a = dict(heads=ha, kv_heads=ha, dk=2 * LANE, dv=vdim, q_col=0, k_col=0, v_col=0)
        spec_b = dict(heads=hb, kv_heads=hkv, dk=hd, dv=hd, q_col=0, k_col=0, v_col=0)
        if not last:
            o_a = _attention(q_a, k_a, v_a, o_a, q_tiles="ctx", name="mla_ctx_attention", **spec_a, **common)
            o_b = _attention(q_b, k_b, v_b, o_b, q_tiles="ctx", name="gqa_ctx_attention", **spec_b, **common)
            o_c = _attention(qkv_c, qkv_c, qkv_c, o_c, heads=hc, kv_heads=hc, dk=hd, dv=hd,
                             q_col=0, k_col=hc, v_col=2 * hc, q_tiles="ctx", name="natten_ctx_attention",
                             **common)
        o_a = _attention(q_a, k_a, v_a, o_a, q_tiles="lat", name="mla_attention", **spec_a, **common)
        o_b = _attention(q_b, k_b, v_b, o_b, q_tiles="lat", name="gqa_attention", **spec_b, **common)
        bias = _natten_bias(rpb[layer], cfg.grid_w, kh, kw)
        o_c = _natten(qkv_c, o_c, bias, batch=batch, n_ctx=n_ctx, n_lat=n_lat, heads=hc, hd=hd,
                      grid_w=cfg.grid_w, kh=kh, name="natten")

        tn_m = _divisor_tile(d, 4 * LANE, LANE)
        merged = _merge(h, o_a, o_b, o_c, w_g, w_b, tm=tm, tn=tn_m)
        gate1 = table(layer, (2,))
        x1 = _matmul(merged, w_o, functools.partial(_epi_residual, tiles_per_batch=tpb, tm=tm, n_ctx=n_ctx),
                     [(xa, pl.BlockSpec((tm, tn_m), lambda i, j: (i, j))),
                      (gate1, pl.BlockSpec((1, 8, tn_m), lambda i, j: (i // tpb, 0, j)))],
                     jax.ShapeDtypeStruct((n_rows, d), F32),
                     pl.BlockSpec((tm, tn_m), lambda i, j: (i, j)), tm=tm, tn=tn_m, name="out_proj_residual")

        h2, top_i, top_w = _norm_mod(x1, None, table(layer, (3, 4)), g_ffn[layer],
                                     (w_router[layer], b_router[layer]), t_rows=t_rows, n_ctx=n_ctx, cfg=cfg)
        valid = is_lat if last else jnp.ones((n_rows,), bool)
        pend_f = _moe(h2, top_i, top_w, valid, w1g, w1l, b1g, b1l, w2, b2, cfg=cfg)
        pend_tab = table(layer, (5,))
        xa = x1

    gate2 = pend_tab[:, 1:2]
    return _final_norm(xa, pend_f, gate2, g_final, batch=batch, n_ctx=n_ctx, n_lat=n_lat, eps=cfg.eps)


def kernel(x, c, ctx, c_ctx, w_mod, b_mod, g_mix, w_in, g_q_a, w_uq, g_kv_a, w_ukv, g_qn, g_kn, rpb,
           w_branch, w_out, g_ffn, w_router, b_router, w_exp1, b_exp1, w_exp2, b_exp2, g_final):
    return _forward(x, c, ctx, c_ctx, w_mod, b_mod, g_mix, w_in, g_q_a, w_uq, g_kv_a, w_ukv, g_qn, g_kn,
                    rpb, w_branch, w_out, g_ffn, w_router, b_router, w_exp1, b_exp1, w_exp2, b_exp2,
                    g_final, CFG)
```

```python
import functools
from typing import NamedTuple

import numpy as np
import jax
import jax.numpy as jnp
from jax import lax
from jax.experimental import pallas as pl
from jax.experimental.pallas import tpu as pltpu

F32 = jnp.float32
BF16 = jnp.bfloat16

LANE = 128
VMEM_LIMIT = 56 * 1024 * 1024
NEG = -1e30


class Config(NamedTuple):
    grid_w: int = 64
    eps: float = 1e-6
    rope_theta: float = 10000.0
    mla_heads: int = 8
    mla_nope: int = 128
    mla_rope: int = 64
    mla_v: int = 128
    gqa_heads: int = 8
    gqa_kv_heads: int = 2
    head_dim: int = 128
    na_heads: int = 8
    top_k: int = 4
    swiglu_alpha: float = 1.702
    swiglu_limit: float = 7.0
    row_tile: int = 640
    kv_tile: int = 512
    attn_heads_per_step: int = 4
    moe_rows: int = 256
    gather_rows: int = 512
    combine_rows: int = 128


CFG = Config()


def _divisor_tile(n, target, mult):
    best = None
    for d in range(mult, min(n, target) + 1, mult):
        if n % d == 0:
            best = d
    assert best is not None, (n, target, mult)
    return best


def _params(*sem):
    return pltpu.CompilerParams(dimension_semantics=sem, vmem_limit_bytes=VMEM_LIMIT)


def _mod_kernel(c_ref, w_ref, b_ref, o_ref):
    c = c_ref[...]
    a = (c * jax.nn.sigmoid(c)).astype(BF16)
    o_ref[0] = jnp.dot(a, w_ref[0].astype(BF16), preferred_element_type=F32) + b_ref[0]


def _modulation(cc, w_mod, b_mod):
    depth, d, n6 = w_mod.shape
    tn = _divisor_tile(n6, 1024, LANE)
    return pl.pallas_call(
        _mod_kernel,
        out_shape=jax.ShapeDtypeStruct((depth, 8, n6), F32),
        grid=(depth, n6 // tn),
        in_specs=[pl.BlockSpec((8, d), lambda l, j: (0, 0)),
                  pl.BlockSpec((1, d, tn), lambda l, j: (l, 0, j)),
                  pl.BlockSpec((1, 1, tn), lambda l, j: (l, 0, j))],
        out_specs=pl.BlockSpec((1, 8, tn), lambda l, j: (l, 0, j)),
        compiler_params=_params("arbitrary", "arbitrary"),
        name="adaln_modulation",
    )(cc, w_mod, b_mod.reshape(depth, 1, n6))


def _ctx_mask(tile_idx, tiles_per_batch, tm, n_ctx):
    row = (tile_idx % tiles_per_batch) * tm + lax.broadcasted_iota(jnp.int32, (tm, 1), 0)
    return row < n_ctx


def _norm_kernel(*refs, has_res, has_router, tiles_per_batch, tm, n_ctx, eps, top_k):
    it = iter(refs)
    x_ref = next(it)
    f_ref = next(it) if has_res else None
    tab_ref = next(it)
    g_ref = next(it)
    if has_router:
        wr_ref = next(it)
        br_ref = next(it)
    xo_ref = next(it) if has_res else None
    h_ref = next(it)
    if has_router:
        ti_ref = next(it)
        tw_ref = next(it)

    is_ctx = _ctx_mask(pl.program_id(0), tiles_per_batch, tm, n_ctx)
    tab = tab_ref[0]

    def pick(k):
        return jnp.where(is_ctx, tab[2 * k:2 * k + 1], tab[2 * k + 1:2 * k + 2])

    x = x_ref[...]
    if has_res:
        x = x + pick(2) * f_ref[...]
        xo_ref[...] = x
    ms = jnp.mean(x * x, axis=-1, keepdims=True)
    y = x * lax.rsqrt(ms + eps) * g_ref[...]
    n = y * (1.0 + pick(1)) + pick(0)
    h_ref[...] = n.astype(BF16)

    if has_router:
        logits = jnp.dot(n, wr_ref[...], preferred_element_type=F32,
                         precision=lax.Precision.HIGHEST) + br_ref[...]
        lane = lax.broadcasted_iota(jnp.int32, logits.shape, 1)
        vals, idxs = [], []
        cur = logits
        for _ in range(top_k):
            m = jnp.max(cur, axis=-1, keepdims=True)
            idx = jnp.min(jnp.where(cur == m, lane, LANE), axis=-1, keepdims=True)
            vals.append(m)
            idxs.append(idx)
            cur = jnp.where(lane == idx, NEG, cur)
        es = [jnp.exp(v - vals[0]) for v in vals]
        den = es[0]
        for e in es[1:]:
            den = den + e
        ti = jnp.zeros(logits.shape, jnp.int32)
        tw = jnp.zeros(logits.shape, F32)
        for k in range(top_k):
            ti = jnp.where(lane == k, idxs[k], ti)
            tw = jnp.where(lane == k, es[k] / den, tw)
        ti_ref[...] = ti
        tw_ref[...] = tw


def _norm_mod(x, f, tab, g, router, *, t_rows, n_ctx, cfg):
    n_rows, d = x.shape
    tm = _divisor_tile(t_rows, 256, 8)
    tpb = t_rows // tm
    has_res = f is not None
    has_router = router is not None
    row = pl.BlockSpec((tm, d), lambda i: (i, 0))
    ins = [x] + ([f] if has_res else []) + [tab, g.reshape(1, d)]
    in_specs = [row] + ([row] if has_res else []) + [
        pl.BlockSpec((1, 8, d), lambda i: (i // tpb, 0, 0)),
        pl.BlockSpec((1, d), lambda i: (0, 0))]
    outs, out_specs = [], []
    if has_res:
        outs.append(jax.ShapeDtypeStruct((n_rows, d), F32))
        out_specs.append(row)
    outs.append(jax.ShapeDtypeStruct((n_rows, d), BF16))
    out_specs.append(row)
    if has_router:
        w_r, b_r = router
        n_exp = w_r.shape[1]
        assert n_exp <= LANE
        wr = jnp.zeros((d, LANE), F32).at[:, :n_exp].set(w_r)
        br = jnp.full((1, LANE), NEG, F32).at[0, :n_exp].set(b_r)
        ins += [wr, br]
        in_specs += [pl.BlockSpec((d, LANE), lambda i: (0, 0)), pl.BlockSpec((1, LANE), lambda i: (0, 0))]
        lane_row = pl.BlockSpec((tm, LANE), lambda i: (i, 0))
        outs += [jax.ShapeDtypeStruct((n_rows, LANE), jnp.int32), jax.ShapeDtypeStruct((n_rows, LANE), F32)]
        out_specs += [lane_row, lane_row]
    res = pl.pallas_call(
        functools.partial(_norm_kernel, has_res=has_res, has_router=has_router, tiles_per_batch=tpb,
                          tm=tm, n_ctx=n_ctx, eps=cfg.eps, top_k=cfg.top_k),
        out_shape=outs, grid=(n_rows // tm,), in_specs=in_specs, out_specs=out_specs,
        compiler_params=_params("arbitrary"),
        name="norm_modulate_router" if has_router else "norm_modulate",
    )(*ins)
    return list(res)


def _rope(t, cos, sin):
    return t * cos + pltpu.roll(t, LANE // 2, 1) * sin


def _mm_kernel(*refs, epilogue, n_extra):
    a_ref, w_ref = refs[0], refs[1]
    extra = refs[2:2 + n_extra]
    outs = refs[2 + n_extra:]
    acc = jnp.dot(a_ref[...], w_ref[...], preferred_element_type=F32)
    epilogue(acc, extra, outs)


def _matmul(a, w, epilogue, extras, out_shape, out_spec, *, tm, tn, a_col=0, name):
    m = a.shape[0]
    k, nw = w.shape
    in_specs = [pl.BlockSpec((tm, k), lambda i, j: (i, a_col)), pl.BlockSpec((k, tn), lambda i, j: (0, j))]
    in_specs += [s for _, s in extras]
    return pl.pallas_call(
        functools.partial(_mm_kernel, epilogue=epilogue, n_extra=len(extras)),
        out_shape=out_shape, grid=(m // tm, nw // tn), in_specs=in_specs, out_specs=out_spec,
        compiler_params=_params("arbitrary", "arbitrary"), name=name,
    )(a, w, *[x for x, _ in extras])


def _epi_rmsnorm(acc, extra, outs, *, eps):
    (g_ref,), (o_ref,) = extra, outs
    ms = jnp.mean(acc * acc, axis=-1, keepdims=True)
    o_ref[...] = (acc * lax.rsqrt(ms + eps) * g_ref[...]).astype(o_ref.dtype)


def _epi_scale(acc, extra, outs):
    (s_ref,), (o_ref,) = extra, outs
    o_ref[...] = (acc * s_ref[...]).astype(o_ref.dtype)


def _epi_plain(acc, extra, outs):
    outs[0][...] = acc.astype(outs[0].dtype)


def _epi_rope_tiles(acc, extra, outs, *, rope_tiles, norm, eps, scale):
    if norm:
        g_ref, cos_ref, sin_ref = extra
    else:
        cos_ref, sin_ref = extra
    o_ref = outs[0]
    cos, sin = cos_ref[...], sin_ref[...]
    for t in range(acc.shape[1] // LANE):
        v = acc[:, t * LANE:(t + 1) * LANE]
        if norm:
            ms = jnp.mean(v * v, axis=-1, keepdims=True)
            v = v * lax.rsqrt(ms + eps) * g_ref[...]
        if rope_tiles[t % len(rope_tiles)]:
            v = _rope(v, cos, sin)
        o_ref[:, t * LANE:(t + 1) * LANE] = (v * scale).astype(o_ref.dtype)


def _epi_mla_key(acc, extra, outs):
    (kpe_ref,), (o_ref,) = extra, outs
    o_ref[:, :LANE] = acc.astype(o_ref.dtype)
    o_ref[:, LANE:] = kpe_ref[...]


def _epi_residual(acc, extra, outs, *, tiles_per_batch, tm, n_ctx):
    (x_ref, tab_ref), (o_ref,) = extra, outs
    is_ctx = _ctx_mask(pl.program_id(0), tiles_per_batch, tm, n_ctx)
    tab = tab_ref[0]
    gate = jnp.where(is_ctx, tab[0:1], tab[1:2])
    o_ref[...] = x_ref[...] + gate * acc


def _attn_kernel(q_ref, k_ref, v_ref, o_ref, *, n_ctx, n_steps, tk, hps, kvps, dk, dv):
    tq = q_ref.shape[0]
    qs = [q_ref[:, i * dk:(i + 1) * dk] for i in range(hps)]

    def step(start, size, state):
        new = []
        for i, (m, l, acc) in enumerate(state):
            kv = i * kvps // hps
            k = k_ref[pl.ds(start, size), kv * dk:(kv + 1) * dk]
            v = v_ref[pl.ds(start, size), kv * dv:(kv + 1) * dv]
            s = lax.dot_general(qs[i], k, (((1,), (1,)), ((), ())), preferred_element_type=F32)
            m_new = jnp.maximum(m, jnp.max(s, axis=-1, keepdims=True))
            alpha = jnp.exp(m - m_new)
            p = jnp.exp(s - m_new)
            l = alpha * l + jnp.sum(p, axis=-1, keepdims=True)
            acc = alpha * acc + jnp.dot(p.astype(BF16), v, preferred_element_type=F32)
            new.append((m_new, l, acc))
        return tuple(new)

    state = tuple((jnp.full((tq, 1), NEG, F32), jnp.zeros((tq, 1), F32), jnp.zeros((tq, dv), F32))
                  for _ in range(hps))
    state = step(0, n_ctx, state)

    def body(c, state):
        return step(pl.multiple_of(n_ctx + c * tk, n_ctx), tk, state)

    state = lax.fori_loop(0, n_steps, body, state)
    for i, (_, l, acc) in enumerate(state):
        o_ref[:, i * dv:(i + 1) * dv] = (acc / l).astype(o_ref.dtype)


def _attention(q, k, v, out, *, batch, n_ctx, n_lat, heads, kv_heads, dk, dv, q_col, k_col, v_col,
               q_tiles, cfg, name):
    t_rows = n_ctx + n_lat
    tq = n_ctx
    tpb = t_rows // tq
    q_off = 1 if q_tiles == "lat" else 0
    n_q = tpb - 1 if q_tiles == "lat" else 1
    group = heads // kv_heads
    tk = _divisor_tile(n_lat, cfg.kv_tile, tq)
    n_steps = n_lat // tk if q_tiles == "lat" else 0
    k_rows = t_rows if q_tiles == "lat" else n_ctx
    kpb = t_rows // k_rows
    hps = cfg.attn_heads_per_step
    assert heads % hps == 0 and q_col % hps == 0
    if group == 1:
        kvps = hps
        assert k_col % hps == 0 and v_col % hps == 0

        def kv_block(col, h):
            return col // hps + h
    else:
        kvps = 1
        assert group % hps == 0

        def kv_block(col, h):
            return col + (h * hps) // group

    def kern(q_ref, k_ref, v_ref, _, o_ref):
        _attn_kernel(q_ref, k_ref, v_ref, o_ref, n_ctx=n_ctx, n_steps=n_steps, tk=tk, hps=hps, kvps=kvps,
                     dk=dk, dv=dv)

    return pl.pallas_call(
        kern, out_shape=jax.ShapeDtypeStruct(out.shape, out.dtype),
        grid=(batch, heads // hps, n_q),
        in_specs=[pl.BlockSpec((tq, hps * dk), lambda b, h, i: (b * tpb + i + q_off, q_col // hps + h)),
                  pl.BlockSpec((k_rows, kvps * dk), lambda b, h, i: (b * kpb, kv_block(k_col, h))),
                  pl.BlockSpec((k_rows, kvps * dv), lambda b, h, i: (b * kpb, kv_block(v_col, h))),
                  pl.BlockSpec(memory_space=pl.ANY)],
        out_specs=pl.BlockSpec((tq, hps * dv), lambda b, h, i: (b * tpb + i + q_off, h)),
        input_output_aliases={3: 0},
        compiler_params=_params("arbitrary", "arbitrary", "arbitrary"), name=name,
    )(q, k, v, out)


def _natten_kernel(q_ref, k_ref, v_ref, bias_ref, o_ref, *, n_ctx, grid_w, rows, kh):
    r = pl.program_id(2)
    r0 = jnp.clip(r - kh // 2, 0, rows - kh)
    start = pl.multiple_of(n_ctx + r0 * grid_w, grid_w)
    q = q_ref[...]
    dn = (((1,), (1,)), ((), ()))
    k_loc = k_ref[pl.ds(start, kh * grid_w), :]
    v_loc = v_ref[pl.ds(start, kh * grid_w), :]
    s_loc = lax.dot_general(q, k_loc, dn, preferred_element_type=F32) + bias_ref[0, 0]
    s_ctx = lax.dot_general(q, k_ref[pl.ds(0, n_ctx), :], dn, preferred_element_type=F32)
    m = jnp.maximum(jnp.max(s_loc, axis=-1, keepdims=True), jnp.max(s_ctx, axis=-1, keepdims=True))
    p_loc = jnp.exp(s_loc - m)
    p_ctx = jnp.exp(s_ctx - m)
    l = jnp.sum(p_loc, axis=-1, keepdims=True) + jnp.sum(p_ctx, axis=-1, keepdims=True)
    acc = jnp.dot(p_loc.astype(BF16), v_loc, preferred_element_type=F32)
    acc = acc + jnp.dot(p_ctx.astype(BF16), v_ref[pl.ds(0, n_ctx), :], preferred_element_type=F32)
    o_ref[...] = (acc / l).astype(o_ref.dtype)


def _natten_bias(rpb, grid_w, kh, kw):
    heads = rpb.shape[0]
    full_kh = (rpb.shape[1] + 1) // 2
    col = np.arange(grid_w)
    col0 = np.clip(col - kw // 2, 0, grid_w - kw)
    kc = np.arange(grid_w)
    inside = (kc[None, :] >= col0[:, None]) & (kc[None, :] < col0[:, None] + kw)
    dc = np.clip(kc[None, :] - col[:, None] + kw - 1, 0, 2 * kw - 2)
    d = np.arange(kh)
    i = np.arange(kh)
    dr = np.clip(i[None, :] - d[:, None] + full_kh - 1, 0, 2 * full_kh - 2)
    b = rpb[:, dr[:, None, :, None], dc[None, :, None, :]]
    b = jnp.where(inside[None, None, :, None, :], b, NEG)
    return b.reshape(heads, kh, grid_w, kh * grid_w).astype(F32)


def _natten(qkv, out, bias, *, batch, n_ctx, n_lat, heads, hd, grid_w, kh, name):
    t_rows = n_ctx + n_lat
    rows = n_lat // grid_w
    assert n_ctx % grid_w == 0 and t_rows % grid_w == 0
    rpb_t = t_rows // grid_w
    off = n_ctx // grid_w
    o_spec = pl.BlockSpec((grid_w, hd), lambda b, h, r: (b * rpb_t + off + r, h))

    def bias_idx(b, h, r):
        return (h, r - jnp.clip(r - kh // 2, 0, rows - kh), 0, 0)

    def kern(q_ref, k_ref, v_ref, bias_ref, _, o_ref):
        _natten_kernel(q_ref, k_ref, v_ref, bias_ref, o_ref, n_ctx=n_ctx, grid_w=grid_w, rows=rows, kh=kh)

    return pl.pallas_call(
        kern, out_shape=jax.ShapeDtypeStruct(out.shape, out.dtype),
        grid=(batch, heads, rows),
        in_specs=[pl.BlockSpec((grid_w, hd), lambda b, h, r: (b * rpb_t + off + r, h)),
                  pl.BlockSpec((t_rows, hd), lambda b, h, r: (b, heads + h)),
                  pl.BlockSpec((t_rows, hd), lambda b, h, r: (b, 2 * heads + h)),
                  pl.BlockSpec((1, 1, grid_w, kh * grid_w), bias_idx),
                  pl.BlockSpec(memory_space=pl.ANY)],
        out_specs=o_spec, input_output_aliases={4: 0},
        compiler_params=_params("arbitrary", "arbitrary", "arbitrary"), name=name,
    )(qkv, qkv, qkv, bias, out)


def _merge_kernel(h_ref, oa_ref, ob_ref, oc_ref, wg_ref, wb_ref, o_ref):
    h = h_ref[...]
    acc = None
    for i, o in enumerate((oa_ref, ob_ref, oc_ref)):
        gate = jax.nn.sigmoid(jnp.dot(h, wg_ref[i], preferred_element_type=F32))
        term = gate * jnp.dot(o[...], wb_ref[i], preferred_element_type=F32)
        acc = term if acc is None else acc + term
    o_ref[...] = acc.astype(o_ref.dtype)


def _merge(h, o_a, o_b, o_c, wg, wb, *, tm, tn):
    n_rows, d = h.shape
    bw = o_a.shape[1]
    nb = wg.shape[0]
    return pl.pallas_call(
        _merge_kernel, out_shape=jax.ShapeDtypeStruct((n_rows, d), BF16),
        grid=(n_rows // tm, d // tn),
        in_specs=[pl.BlockSpec((tm, d), lambda i, j: (i, 0))]
        + [pl.BlockSpec((tm, bw), lambda i, j: (i, 0))] * 3
        + [pl.BlockSpec((nb, d, tn), lambda i, j: (0, 0, j)),
           pl.BlockSpec((nb, bw, tn), lambda i, j: (0, 0, j))],
        out_specs=pl.BlockSpec((tm, tn), lambda i, j: (i, j)),
        compiler_params=_params("arbitrary", "arbitrary"), name="gated_merge",
    )(h, o_a, o_b, o_c, wg, wb)


def _deinterleave_kernel(w_ref, sel_ref, o_ref):
    sel = sel_ref[...]
    for c in range(w_ref.shape[2] // (2 * LANE)):
        sl = slice(c * 2 * LANE, (c + 1) * 2 * LANE)
        blk = w_ref[0, :, sl].astype(BF16)
        o_ref[0, :, sl] = jnp.dot(blk, sel, preferred_element_type=F32).astype(BF16)


def _deinterleave_up_weights(w1):
    n_exp, d, f2 = w1.shape
    assert f2 % (2 * LANE) == 0
    tr = _divisor_tile(d, 256, 8)
    sel = np.zeros((2 * LANE, 2 * LANE), np.float32)
    sel[2 * np.arange(LANE), np.arange(LANE)] = 1.0
    sel[2 * np.arange(LANE) + 1, LANE + np.arange(LANE)] = 1.0
    return pl.pallas_call(
        _deinterleave_kernel, out_shape=jax.ShapeDtypeStruct(w1.shape, BF16),
        grid=(n_exp, d // tr),
        in_specs=[pl.BlockSpec((1, tr, f2), lambda e, r: (e, r, 0)),
                  pl.BlockSpec((2 * LANE, 2 * LANE), lambda e, r: (0, 0))],
        out_specs=pl.BlockSpec((1, tr, f2), lambda e, r: (e, r, 0)),
        compiler_params=_params("arbitrary", "arbitrary"), name="moe_weight_deinterleave",
    )(w1, jnp.asarray(sel, BF16))


def _dispatch_kernel(dest_ref, src_ref, _, dst_ref, sem, *, tm, top_k):
    base = pl.program_id(0) * tm

    def copies(t):
        for k in range(top_k):
            d = dest_ref[(base + t) * top_k + k]
            yield d, pltpu.make_async_copy(src_ref.at[t], dst_ref.at[jnp.maximum(d, 0)], sem)

    def issue(t, _):
        for d, cp in copies(t):
            pl.when(d >= 0)(cp.start)
        return 0

    def drain(t, _):
        for d, cp in copies(t):
            pl.when(d >= 0)(cp.wait)
        return 0

    lax.fori_loop(0, tm, issue, 0)
    lax.fori_loop(0, tm, drain, 0)


def _dispatch(dest, h3, xg0, *, tm, top_k):
    n_tok, s, lanes = h3.shape
    return pl.pallas_call(
        functools.partial(_dispatch_kernel, tm=tm, top_k=top_k),
        out_shape=jax.ShapeDtypeStruct(xg0.shape, xg0.dtype),
        grid_spec=pltpu.PrefetchScalarGridSpec(
            num_scalar_prefetch=1, grid=(n_tok // tm,),
            in_specs=[pl.BlockSpec((tm, s, lanes), lambda i, dest: (i, 0, 0)),
                      pl.BlockSpec(memory_space=pl.ANY)],
            out_specs=pl.BlockSpec(memory_space=pl.ANY),
            scratch_shapes=[pltpu.SemaphoreType.DMA]),
        input_output_aliases={2: 0},
        compiler_params=pltpu.CompilerParams(dimension_semantics=("arbitrary",),
                                             vmem_limit_bytes=VMEM_LIMIT, disable_bounds_checks=True),
        name="moe_dispatch",
    )(dest, h3, xg0)


def _swiglu(x_glu, x_lin, alpha, limit):
    x_glu = jnp.minimum(x_glu, limit)
    x_lin = jnp.clip(x_lin, -limit, limit)
    return x_glu * jax.nn.sigmoid(alpha * x_glu) * (x_lin + 1.0)


def _moe_up_kernel(be_ref, nu_ref, x_ref, w_ref, b_ref, a_ref, *, tc, alpha, limit):
    i = pl.program_id(0)

    @pl.when(i < nu_ref[0])
    def _():
        x = x_ref[...]
        for c in range(w_ref.shape[2] // tc):
            sl = slice(c * tc, (c + 1) * tc)
            hdn = jnp.dot(x, w_ref[0, :, sl], preferred_element_type=F32) + b_ref[0, :, sl]
            for j in range(tc // (2 * LANE)):
                g = hdn[:, 2 * j * LANE:(2 * j + 1) * LANE]
                lin = hdn[:, (2 * j + 1) * LANE:(2 * j + 2) * LANE]
                col = (c * tc // (2 * LANE) + j) * LANE
                a_ref[:, col:col + LANE] = _swiglu(g, lin, alpha, limit).astype(a_ref.dtype)

    @pl.when(i >= nu_ref[0])
    def _():
        a_ref[...] = jnp.zeros(a_ref.shape, a_ref.dtype)


def _moe_down_kernel(be_ref, nu_ref, a_ref, w_ref, b_ref, y_ref):
    i = pl.program_id(0)

    @pl.when(i < nu_ref[0])
    def _():
        y_ref[...] = jnp.dot(a_ref[...], w_ref[0], preferred_element_type=F32) + b_ref[0]

    @pl.when(i >= nu_ref[0])
    def _():
        y_ref[...] = jnp.zeros(y_ref.shape, y_ref.dtype)


def _moe_experts(blk_e, n_used, xg, w1, b1, w2, b2, *, bm, cfg):
    p_rows, d = xg.shape
    n_exp, _, f2 = w1.shape
    ff = f2 // 2
    nb = p_rows // bm
    tc = _divisor_tile(f2, 1024, 2 * LANE)
    a = pl.pallas_call(
        functools.partial(_moe_up_kernel, tc=tc, alpha=cfg.swiglu_alpha, limit=cfg.swiglu_limit),
        out_shape=jax.ShapeDtypeStruct((p_rows, ff), BF16),
        grid_spec=pltpu.PrefetchScalarGridSpec(
            num_scalar_prefetch=2, grid=(nb,),
            in_specs=[pl.BlockSpec((bm, d), lambda i, be, nu: (i, 0)),
                      pl.BlockSpec((1, d, f2), lambda i, be, nu: (be[i], 0, 0)),
                      pl.BlockSpec((1, 1, f2), lambda i, be, nu: (be[i], 0, 0))],
            out_specs=pl.BlockSpec((bm, ff), lambda i, be, nu: (i, 0))),
        compiler_params=_params("arbitrary"), name="moe_up_swiglu",
    )(blk_e, n_used, xg, w1, b1)
    return pl.pallas_call(
        _moe_down_kernel,
        out_shape=jax.ShapeDtypeStruct((p_rows, d), F32),
        grid_spec=pltpu.PrefetchScalarGridSpec(
            num_scalar_prefetch=2, grid=(nb,),
            in_specs=[pl.BlockSpec((bm, ff), lambda i, be, nu: (i, 0)),
                      pl.BlockSpec((1, ff, d), lambda i, be, nu: (be[i], 0, 0)),
                      pl.BlockSpec((1, 1, d), lambda i, be, nu: (be[i], 0, 0))],
            out_specs=pl.BlockSpec((bm, d), lambda i, be, nu: (i, 0))),
        compiler_params=_params("arbitrary"), name="moe_down",
    )(blk_e, n_used, a, w2, b2)


def _combine_kernel(dest_ref, y_ref, w_ref, o_ref, buf, sem, *, tm, top_k):
    base = pl.program_id(0) * tm

    def copies(t):
        for k in range(top_k):
            yield pltpu.make_async_copy(y_ref.at[dest_ref[(base + t) * top_k + k]], buf.at[k, t], sem)

    def issue(t, _):
        for cp in copies(t):
            cp.start()
        return 0

    def drain(t, _):
        for cp in copies(t):
            cp.wait()
        return 0

    lax.fori_loop(0, tm, issue, 0)
    lax.fori_loop(0, tm, drain, 0)
    acc = buf[0] * w_ref[0]
    for k in range(1, top_k):
        acc = acc + buf[k] * w_ref[k]
    o_ref[...] = acc


def _combine(dest, y3, w_rep, *, tm, top_k):
    _, s, lanes = y3.shape
    n_tok = w_rep.shape[1]
    return pl.pallas_call(
        functools.partial(_combine_kernel, tm=tm, top_k=top_k),
        out_shape=jax.ShapeDtypeStruct((n_tok, s, lanes), F32),
        grid_spec=pltpu.PrefetchScalarGridSpec(
            num_scalar_prefetch=1, grid=(n_tok // tm,),
            in_specs=[pl.BlockSpec(memory_space=pl.ANY),
                      pl.BlockSpec((top_k, tm, 1, lanes), lambda i, dest: (0, i, 0, 0))],
            out_specs=pl.BlockSpec((tm, s, lanes), lambda i, dest: (i, 0, 0)),
            scratch_shapes=[pltpu.VMEM((top_k, tm, s, lanes), F32), pltpu.SemaphoreType.DMA]),
        compiler_params=pltpu.CompilerParams(dimension_semantics=("arbitrary",),
                                             vmem_limit_bytes=VMEM_LIMIT, disable_bounds_checks=True),
        name="moe_combine",
    )(dest, y3, w_rep)


def _moe(h, top_i, top_w, valid, w1, b1, w2, b2, *, cfg):
    n_tok, d = h.shape
    n_exp = w1.shape[0]
    kk = cfg.top_k
    bm = cfg.moe_rows
    idx = top_i[:, :kk]
    sel = (idx[:, :, None] == jnp.arange(n_exp, dtype=jnp.int32)[None, None, :]) & valid[:, None, None]
    sel = jnp.any(sel, axis=1).astype(jnp.int32)
    counts = jnp.sum(sel, axis=0)
    rank = jnp.cumsum(sel, axis=0) - sel
    padded = (counts + bm - 1) // bm * bm
    gend = jnp.cumsum(padded)
    gstart = gend - padded
    onehot = (idx[:, :, None] == jnp.arange(n_exp, dtype=jnp.int32)[None, None, :]).astype(jnp.int32)
    dest = jnp.sum(onehot * (gstart[None, None, :] + rank[:, None, :]), axis=-1)
    n_blocks = -(-(n_tok * kk) // bm) + n_exp
    p_rows = n_blocks * bm
    starts = jnp.arange(n_blocks, dtype=jnp.int32) * bm
    blk_e = jnp.minimum(jnp.sum(gend[None, :] <= starts[:, None], axis=1), n_exp - 1).astype(jnp.int32)
    n_used = (gend[-1] // bm).astype(jnp.int32).reshape(1)

    s = d // LANE
    dest_w = jnp.where(valid[:, None], dest, -1).astype(jnp.int32).reshape(-1)
    tmd = _divisor_tile(n_tok, cfg.gather_rows, 16)
    xg = _dispatch(dest_w, h.reshape(n_tok, s, LANE), jnp.zeros((p_rows, s, LANE), h.dtype), tm=tmd, top_k=kk)
    y = _moe_experts(blk_e, n_used, xg.reshape(p_rows, d), w1, b1, w2, b2, bm=bm, cfg=cfg)
    dest_r = jnp.where(valid[:, None], dest, 0).astype(jnp.int32).reshape(-1)
    w_rep = jnp.broadcast_to(top_w[:, :kk].T[:, :, None, None], (kk, n_tok, 1, LANE))
    tmc = _divisor_tile(n_tok, cfg.combine_rows, 8)
    f = _combine(dest_r, y.reshape(p_rows, s, LANE), w_rep, tm=tmc, top_k=kk)
    return f.reshape(n_tok, d)


def _final_kernel(x_ref, f_ref, gate_ref, g_ref, o_ref, *, eps):
    x = x_ref[...] + gate_ref[0] * f_ref[...]
    ms = jnp.mean(x * x, axis=-1, keepdims=True)
    o_ref[0] = x * lax.rsqrt(ms + eps) * g_ref[...]


def _final_norm(x, f, gate, g, *, batch, n_ctx, n_lat, eps):
    d = x.shape[1]
    ts = n_ctx
    tpb = (n_ctx + n_lat) // ts
    row = pl.BlockSpec((ts, d), lambda b, s: (b * tpb + 1 + s, 0))
    return pl.pallas_call(
        functools.partial(_final_kernel, eps=eps),
        out_shape=jax.ShapeDtypeStruct((batch, n_lat, d), F32),
        grid=(batch, n_lat // ts),
        in_specs=[row, row, pl.BlockSpec((1, 1, d), lambda b, s: (b, 0, 0)),
                  pl.BlockSpec((1, d), lambda b, s: (0, 0))],
        out_specs=pl.BlockSpec((1, ts, d), lambda b, s: (b, s, 0)),
        compiler_params=_params("arbitrary", "arbitrary"), name="final_norm",
    )(x, f, gate, g.reshape(1, d))


def _rope_tables(kind, n_ctx, n_lat, grid_w, theta, dims):
    half = dims // 2
    nf = half // 2
    freqs = theta ** (-np.arange(0, half, 2, dtype=np.float64) / half)
    t = np.arange(n_lat)
    pos_r = np.concatenate([np.zeros(n_ctx), t // grid_w]).astype(np.float64)
    pos_c = np.concatenate([np.zeros(n_ctx), t % grid_w]).astype(np.float64)
    ang_r = pos_r[:, None] * freqs[None, :]
    ang_c = pos_c[:, None] * freqs[None, :]
    ang = np.concatenate([ang_r, ang_c], axis=1)
    rows = n_ctx + n_lat
    cos = np.ones((rows, LANE), np.float32)
    sin = np.zeros((rows, LANE), np.float32)
    w = 2 * nf
    cos[:, :w] = np.cos(ang)
    cos[:, LANE // 2:LANE // 2 + w] = np.cos(ang)
    sin[:, :w] = -np.sin(ang)
    sin[:, LANE // 2:LANE // 2 + w] = np.sin(ang)
    del kind
    return jnp.asarray(cos), jnp.asarray(sin)


def _axial_lanes(w, dims):
    nf = dims // 4
    lead = w.shape[:-1]
    w = jnp.swapaxes(w.reshape(lead + (2, 2, nf)), -3, -2).reshape(lead + (2, 2 * nf))
    w = jnp.pad(w, [(0, 0)] * (len(lead) + 1) + [(0, LANE // 2 - 2 * nf)])
    return w.reshape(lead + (LANE,))


def _forward(x, c, ctx, c_ctx, w_mod, b_mod, g_mix, w_in, g_q_a, w_uq, g_kv_a, w_ukv, g_qn, g_kn,
             rpb, w_branch, w_out, g_ffn, w_router, b_router, w_exp1, b_exp1, w_exp2, b_exp2, g_final,
             cfg):
    batch, n_lat, d = x.shape
    n_ctx = ctx.shape[1]
    depth = w_mod.shape[0]
    t_rows = n_ctx + n_lat
    n_rows = batch * t_rows
    hd = cfg.head_dim
    q_rank, kv_rank = g_q_a.shape[1], g_kv_a.shape[1]
    ha, hb, hkv, hc = cfg.mla_heads, cfg.gqa_heads, cfg.gqa_kv_heads, cfg.na_heads
    nope, rdim, vdim = cfg.mla_nope, cfg.mla_rope, cfg.mla_v
    assert nope == LANE and vdim == LANE and hd == LANE and rdim <= LANE // 2
    assert n_lat % n_ctx == 0 and n_ctx % 16 == 0 and batch <= 7
    kh_full = (rpb.shape[2] + 1) // 2
    kw = (rpb.shape[3] + 1) // 2
    rows = n_lat // cfg.grid_w
    kh = min(kh_full, rows)
    n_branch = w_branch.shape[1]
    assert n_branch == 3

    tm = _divisor_tile(t_rows, cfg.row_tile, 16)
    tpb = t_rows // tm

    sizes = (q_rank, kv_rank, rdim, hb * hd, hkv * hd, hkv * hd, hc * hd, hc * hd, hc * hd, n_branch * d)
    offs = np.concatenate([[0], np.cumsum(sizes)])

    cos_a, sin_a = _rope_tables("mla", n_ctx, n_lat, cfg.grid_w, cfg.rope_theta, rdim)
    cos_b, sin_b = _rope_tables("gqa", n_ctx, n_lat, cfg.grid_w, cfg.rope_theta, hd)
    rope_spec = pl.BlockSpec((tm, LANE), lambda i, j: (i % tpb, 0))
    mla_scale = float((nope + rdim) ** -0.5)
    head_scale = float(hd ** -0.5)

    cc = jnp.zeros((8, d), F32).at[:batch].set(c).at[batch].set(c_ctx)
    mod = _modulation(cc, w_mod, b_mod)

    def table(layer, chunks):
        rows_ = []
        for ch in chunks:
            sl = mod[layer, :, ch * d:(ch + 1) * d]
            rows_.append(jnp.broadcast_to(sl[batch][None], (batch, d)))
            rows_.append(sl[:batch])
        tab = jnp.stack(rows_, axis=1)
        return jnp.pad(tab, ((0, 0), (0, 8 - tab.shape[1]), (0, 0)))

    xa = jnp.concatenate([ctx, x], axis=1).reshape(n_rows, d)
    pend_f = None
    pend_tab = None
    is_lat = (jnp.arange(n_rows, dtype=jnp.int32) % t_rows) >= n_ctx

    for layer in range(depth):
        last = layer == depth - 1
        wi = w_in[layer]

        w_d = wi[:, offs[0]:offs[2]].astype(BF16)
        g_d = jnp.concatenate([g_q_a[layer], g_kv_a[layer]]).reshape(1, -1)
        w_kr = _axial_lanes(wi[:, offs[2]:offs[3]], rdim).astype(BF16)
        uq = w_uq[layer].reshape(q_rank, ha, nope + rdim)
        uq = jnp.concatenate([uq[:, :, :nope], _axial_lanes(uq[:, :, nope:], rdim)], axis=-1)
        uq = uq.reshape(q_rank, ha * 2 * LANE).astype(BF16)
        ukv = w_ukv[layer].reshape(kv_rank, ha, nope + vdim)
        uk = ukv[:, :, :nope].reshape(kv_rank, ha * nope).astype(BF16)
        uv = ukv[:, :, nope:].reshape(kv_rank, ha * vdim).astype(BF16)
        w_qb = _axial_lanes(wi[:, offs[3]:offs[4]].reshape(d, hb, hd), hd).reshape(d, hb * hd).astype(BF16)
        w_kb = _axial_lanes(wi[:, offs[4]:offs[5]].reshape(d, hkv, hd), hd).reshape(d, hkv * hd).astype(BF16)
        w_vb = wi[:, offs[5]:offs[6]].astype(BF16)
        g_qn_p = _axial_lanes(g_qn[layer], hd).reshape(1, hd)
        g_kn_p = _axial_lanes(g_kn[layer], hd).reshape(1, hd)
        w_c = wi[:, offs[6]:offs[9]].astype(BF16)
        c_scale = jnp.concatenate([jnp.full((hc * hd,), head_scale, F32),
                                   jnp.ones((2 * hc * hd,), F32)]).reshape(1, -1)
        w_g = wi[:, offs[9]:offs[10]].reshape(d, n_branch, d).transpose(1, 0, 2).astype(BF16)
        w_b = w_branch[layer].astype(BF16)
        w_o = w_out[layer].astype(BF16)
        ff = w_exp2.shape[2]
        w1 = _deinterleave_up_weights(w_exp1[layer])
        b1 = b_exp1[layer].reshape(-1, ff // LANE, LANE, 2).transpose(0, 1, 3, 2).reshape(-1, 1, 2 * ff)
        w2 = w_exp2[layer].astype(BF16)
        b2 = b_exp2[layer].reshape(-1, 1, d)

        tab1 = table(layer, (0, 1)) if pend_f is None else jnp.concatenate(
            [table(layer, (0, 1))[:, :4], pend_tab[:, :2], jnp.zeros((batch, 2, d), F32)], axis=1)
        res = _norm_mod(xa, pend_f, tab1, g_mix[layer], None, t_rows=t_rows, n_ctx=n_ctx, cfg=cfg)
        if pend_f is not None:
            xa, h = res
        else:
            (h,) = res

        dn = _matmul(h, w_d, functools.partial(_epi_rmsnorm, eps=cfg.eps),
                     [(g_d, pl.BlockSpec((1, q_rank), lambda i, j: (0, j)))],
                     jax.ShapeDtypeStruct((n_rows, q_rank + kv_rank), BF16),
                     pl.BlockSpec((tm, q_rank), lambda i, j: (i, j)), tm=tm, tn=q_rank, name="mla_down")
        assert q_rank == kv_rank
        k_pe = _matmul(h, w_kr,
                       functools.partial(_epi_rope_tiles, rope_tiles=(True,), norm=False, eps=cfg.eps, scale=1.0),
                       [(cos_a, rope_spec), (sin_a, rope_spec)],
                       jax.ShapeDtypeStruct((n_rows, LANE), BF16),
                       pl.BlockSpec((tm, LANE), lambda i, j: (i, 0)), tm=tm, tn=LANE, name="mla_rope_key")
        q_a = _matmul(dn, uq,
                      functools.partial(_epi_rope_tiles, rope_tiles=(False, True), norm=False, eps=cfg.eps,
                                        scale=mla_scale),
                      [(cos_a, rope_spec), (sin_a, rope_spec)],
                      jax.ShapeDtypeStruct((n_rows, ha * 2 * LANE), BF16),
                      pl.BlockSpec((tm, 2 * LANE), lambda i, j: (i, j)), tm=tm, tn=2 * LANE, a_col=0,
                      name="mla_q_up")
        k_a = _matmul(dn, uk, _epi_mla_key,
                      [(k_pe, pl.BlockSpec((tm, LANE), lambda i, j: (i, 0)))],
                      jax.ShapeDtypeStruct((n_rows, ha * 2 * LANE), BF16),
                      pl.BlockSpec((tm, 2 * LANE), lambda i, j: (i, j)), tm=tm, tn=LANE, a_col=1,
                      name="mla_k_up")
        tn_v = _divisor_tile(ha * vdim, 4 * LANE, LANE)
        v_a = _matmul(dn, uv, _epi_plain, [],
                      jax.ShapeDtypeStruct((n_rows, ha * vdim), BF16),
                      pl.BlockSpec((tm, tn_v), lambda i, j: (i, j)), tm=tm, tn=tn_v, a_col=1, name="mla_v_up")

        tn_q = _divisor_tile(hb * hd, 4 * LANE, LANE)
        q_b = _matmul(h, w_qb,
                      functools.partial(_epi_rope_tiles, rope_tiles=(True,), norm=True, eps=cfg.eps,
                                        scale=head_scale),
                      [(g_qn_p, pl.BlockSpec((1, hd), lambda i, j: (0, 0))), (cos_b, rope_spec), (sin_b, rope_spec)],
                      jax.ShapeDtypeStruct((n_rows, hb * hd), BF16),
                      pl.BlockSpec((tm, tn_q), lambda i, j: (i, j)), tm=tm, tn=tn_q, name="gqa_q")
        tn_k = _divisor_tile(hkv * hd, 4 * LANE, LANE)
        k_b = _matmul(h, w_kb,
                      functools.partial(_epi_rope_tiles, rope_tiles=(True,), norm=True, eps=cfg.eps, scale=1.0),
                      [(g_kn_p, pl.BlockSpec((1, hd), lambda i, j: (0, 0))), (cos_b, rope_spec), (sin_b, rope_spec)],
                      jax.ShapeDtypeStruct((n_rows, hkv * hd), BF16),
                      pl.BlockSpec((tm, tn_k), lambda i, j: (i, j)), tm=tm, tn=tn_k, name="gqa_k")
        v_b = _matmul(h, w_vb, _epi_plain, [],
                      jax.ShapeDtypeStruct((n_rows, hkv * hd), BF16),
                      pl.BlockSpec((tm, tn_k), lambda i, j: (i, j)), tm=tm, tn=tn_k, name="gqa_v")

        tn_c = _divisor_tile(3 * hc * hd, 4 * LANE, LANE)
        qkv_c = _matmul(h, w_c, _epi_scale, [(c_scale, pl.BlockSpec((1, tn_c), lambda i, j: (0, j)))],
                        jax.ShapeDtypeStruct((n_rows, 3 * hc * hd), BF16),
                        pl.BlockSpec((tm, tn_c), lambda i, j: (i, j)), tm=tm, tn=tn_c, name="natten_qkv")

        common = dict(batch=batch, n_ctx=n_ctx, n_lat=n_lat, cfg=cfg)
        assert ha * vdim == hb * hd == hc * hd
        o_a = o_b = o_c = jnp.zeros((n_rows, hc * hd), BF16)
        spec_a = dict(heads=ha, kv_heads=ha, dk=2 * LANE, dv=vdim, q_col=0, k_col=0, v_col=0)
        spec_b = dict(heads=hb, kv_heads=hkv, dk=hd, dv=hd, q_col=0, k_col=0, v_col=0)
        if not last:
            o_a = _attention(q_a, k_a, v_a, o_a, q_tiles="ctx", name="mla_ctx_attention", **spec_a, **common)
            o_b = _attention(q_b, k_b, v_b, o_b, q_tiles="ctx", name="gqa_ctx_attention", **spec_b, **common)
            o_c = _attention(qkv_c, qkv_c, qkv_c, o_c, heads=hc, kv_heads=hc, dk=hd, dv=hd,
                             q_col=0, k_col=hc, v_col=2 * hc, q_tiles="ctx", name="natten_ctx_attention",
                             **common)
        o_a = _attention(q_a, k_a, v_a, o_a, q_tiles="lat", name="mla_attention", **spec_a, **common)
        o_b = _attention(q_b, k_b, v_b, o_b, q_tiles="lat", name="gqa_attention", **spec_b, **common)
        bias = _natten_bias(rpb[layer], cfg.grid_w, kh, kw)
        o_c = _natten(qkv_c, o_c, bias, batch=batch, n_ctx=n_ctx, n_lat=n_lat, heads=hc, hd=hd,
                      grid_w=cfg.grid_w, kh=kh, name="natten")

        tn_m = _divisor_tile(d, 4 * LANE, LANE)
        merged = _merge(h, o_a, o_b, o_c, w_g, w_b, tm=tm, tn=tn_m)
        gate1 = table(layer, (2,))
        x1 = _matmul(merged, w_o, functools.partial(_epi_residual, tiles_per_batch=tpb, tm=tm, n_ctx=n_ctx),
                     [(xa, pl.BlockSpec((tm, tn_m), lambda i, j: (i, j))),
                      (gate1, pl.BlockSpec((1, 8, tn_m), lambda i, j: (i // tpb, 0, j)))],
                     jax.ShapeDtypeStruct((n_rows, d), F32),
                     pl.BlockSpec((tm, tn_m), lambda i, j: (i, j)), tm=tm, tn=tn_m, name="out_proj_residual")

        h2, top_i, top_w = _norm_mod(x1, None, table(layer, (3, 4)), g_ffn[layer],
                                     (w_router[layer], b_router[layer]), t_rows=t_rows, n_ctx=n_ctx, cfg=cfg)
        valid = is_lat if last else jnp.ones((n_rows,), bool)
        pend_f = _moe(h2, top_i, top_w, valid, w1, b1, w2, b2, cfg=cfg)
        pend_tab = table(layer, (5,))
        xa = x1

    gate2 = pend_tab[:, 1:2]
    return _final_norm(xa, pend_f, gate2, g_final, batch=batch, n_ctx=n_ctx, n_lat=n_lat, eps=cfg.eps)


def kernel(x, c, ctx, c_ctx, w_mod, b_mod, g_mix, w_in, g_q_a, w_uq, g_kv_a, w_ukv, g_qn, g_kn, rpb,
           w_branch, w_out, g_ffn, w_router, b_router, w_exp1, b_exp1, w_exp2, b_exp2, g_final):
    return _forward(x, c, ctx, c_ctx, w_mod, b_mod, g_mix, w_in, g_q_a, w_uq, g_kv_a, w_ukv, g_qn, g_kn,
                    rpb, w_branch, w_out, g_ffn, w_router, b_router, w_exp1, b_exp1, w_exp2, b_exp2,
                    g_final, CFG)
```

```python
import functools
from typing import NamedTuple

import numpy as np
import jax
import jax.numpy as jnp
from jax import lax
from jax.experimental import pallas as pl
from jax.experimental.pallas import tpu as pltpu

F32 = jnp.float32
BF16 = jnp.bfloat16

LANE = 128
VMEM_LIMIT = 56 * 1024 * 1024
NEG = -1e30


class Config(NamedTuple):
    grid_w: int = 64
    eps: float = 1e-6
    rope_theta: float = 10000.0
    mla_heads: int = 8
    mla_nope: int = 128
    mla_rope: int = 64
    mla_v: int = 128
    gqa_heads: int = 8
    gqa_kv_heads: int = 2
    head_dim: int = 128
    na_heads: int = 8
    top_k: int = 4
    swiglu_alpha: float = 1.702
    swiglu_limit: float = 7.0
    row_tile: int = 640
    kv_tile: int = 2048
    attn_heads_per_step: int = 4
    moe_rows: int = 256
    gather_rows: int = 512
    combine_rows: int = 128


CFG = Config()


def _divisor_tile(n, target, mult):
    best = None
    for d in range(mult, min(n, target) + 1, mult):
        if n % d == 0:
            best = d
    assert best is not None, (n, target, mult)
    return best


def _params(*sem):
    return pltpu.CompilerParams(dimension_semantics=sem, vmem_limit_bytes=VMEM_LIMIT)


def _mod_kernel(c_ref, w_ref, b_ref, o_ref):
    c = c_ref[...]
    a = (c * jax.nn.sigmoid(c)).astype(BF16)
    o_ref[0] = jnp.dot(a, w_ref[0].astype(BF16), preferred_element_type=F32) + b_ref[0]


def _modulation(cc, w_mod, b_mod):
    depth, d, n6 = w_mod.shape
    tn = _divisor_tile(n6, 1024, LANE)
    return pl.pallas_call(
        _mod_kernel,
        out_shape=jax.ShapeDtypeStruct((depth, 8, n6), F32),
        grid=(depth, n6 // tn),
        in_specs=[pl.BlockSpec((8, d), lambda l, j: (0, 0)),
                  pl.BlockSpec((1, d, tn), lambda l, j: (l, 0, j)),
                  pl.BlockSpec((1, 1, tn), lambda l, j: (l, 0, j))],
        out_specs=pl.BlockSpec((1, 8, tn), lambda l, j: (l, 0, j)),
        compiler_params=_params("arbitrary", "arbitrary"),
        name="adaln_modulation",
    )(cc, w_mod, b_mod.reshape(depth, 1, n6))


def _ctx_mask(tile_idx, tiles_per_batch, tm, n_ctx):
    row = (tile_idx % tiles_per_batch) * tm + lax.broadcasted_iota(jnp.int32, (tm, 1), 0)
    return row < n_ctx


def _norm_kernel(*refs, has_res, has_router, tiles_per_batch, tm, n_ctx, eps, top_k):
    it = iter(refs)
    x_ref = next(it)
    f_ref = next(it) if has_res else None
    tab_ref = next(it)
    g_ref = next(it)
    if has_router:
        wr_ref = next(it)
        br_ref = next(it)
    xo_ref = next(it) if has_res else None
    h_ref = next(it)
    if has_router:
        ti_ref = next(it)
        tw_ref = next(it)

    is_ctx = _ctx_mask(pl.program_id(0), tiles_per_batch, tm, n_ctx)
    tab = tab_ref[0]

    def pick(k):
        return jnp.where(is_ctx, tab[2 * k:2 * k + 1], tab[2 * k + 1:2 * k + 2])

    x = x_ref[...]
    if has_res:
        x = x + pick(2) * f_ref[...]
        xo_ref[...] = x
    ms = jnp.mean(x * x, axis=-1, keepdims=True)
    y = x * lax.rsqrt(ms + eps) * g_ref[...]
    n = y * (1.0 + pick(1)) + pick(0)
    h_ref[...] = n.astype(BF16)

    if has_router:
        logits = jnp.dot(n, wr_ref[...], preferred_element_type=F32,
                         precision=lax.Precision.HIGHEST) + br_ref[...]
        lane = lax.broadcasted_iota(jnp.int32, logits.shape, 1)
        vals, idxs = [], []
        cur = logits
        for _ in range(top_k):
            m = jnp.max(cur, axis=-1, keepdims=True)
            idx = jnp.min(jnp.where(cur == m, lane, LANE), axis=-1, keepdims=True)
            vals.append(m)
            idxs.append(idx)
            cur = jnp.where(lane == idx, NEG, cur)
        es = [jnp.exp(v - vals[0]) for v in vals]
        den = es[0]
        for e in es[1:]:
            den = den + e
        ti = jnp.zeros(logits.shape, jnp.int32)
        tw = jnp.zeros(logits.shape, F32)
        for k in range(top_k):
            ti = jnp.where(lane == k, idxs[k], ti)
            tw = jnp.where(lane == k, es[k] / den, tw)
        ti_ref[...] = ti
        tw_ref[...] = tw


def _norm_mod(x, f, tab, g, router, *, t_rows, n_ctx, cfg):
    n_rows, d = x.shape
    tm = _divisor_tile(t_rows, 256, 8)
    tpb = t_rows // tm
    has_res = f is not None
    has_router = router is not None
    row = pl.BlockSpec((tm, d), lambda i: (i, 0))
    ins = [x] + ([f] if has_res else []) + [tab, g.reshape(1, d)]
    in_specs = [row] + ([row] if has_res else []) + [
        pl.BlockSpec((1, 8, d), lambda i: (i // tpb, 0, 0)),
        pl.BlockSpec((1, d), lambda i: (0, 0))]
    outs, out_specs = [], []
    if has_res:
        outs.append(jax.ShapeDtypeStruct((n_rows, d), F32))
        out_specs.append(row)
    outs.append(jax.ShapeDtypeStruct((n_rows, d), BF16))
    out_specs.append(row)
    if has_router:
        w_r, b_r = router
        n_exp = w_r.shape[1]
        assert n_exp <= LANE
        wr = jnp.zeros((d, LANE), F32).at[:, :n_exp].set(w_r)
        br = jnp.full((1, LANE), NEG, F32).at[0, :n_exp].set(b_r)
        ins += [wr, br]
        in_specs += [pl.BlockSpec((d, LANE), lambda i: (0, 0)), pl.BlockSpec((1, LANE), lambda i: (0, 0))]
        lane_row = pl.BlockSpec((tm, LANE), lambda i: (i, 0))
        outs += [jax.ShapeDtypeStruct((n_rows, LANE), jnp.int32), jax.ShapeDtypeStruct((n_rows, LANE), F32)]
        out_specs += [lane_row, lane_row]
    res = pl.pallas_call(
        functools.partial(_norm_kernel, has_res=has_res, has_router=has_router, tiles_per_batch=tpb,
                          tm=tm, n_ctx=n_ctx, eps=cfg.eps, top_k=cfg.top_k),
        out_shape=outs, grid=(n_rows // tm,), in_specs=in_specs, out_specs=out_specs,
        compiler_params=_params("arbitrary"),
        name="norm_modulate_router" if has_router else "norm_modulate",
    )(*ins)
    return list(res)


def _rope(t, cos, sin):
    return t * cos + pltpu.roll(t, LANE // 2, 1) * sin


def _mm_kernel(*refs, epilogue, n_extra):
    a_ref, w_ref = refs[0], refs[1]
    extra = refs[2:2 + n_extra]
    outs = refs[2 + n_extra:]
    acc = jnp.dot(a_ref[...], w_ref[...], preferred_element_type=F32)
    epilogue(acc, extra, outs)


def _matmul(a, w, epilogue, extras, out_shape, out_spec, *, tm, tn, a_col=0, name):
    m = a.shape[0]
    k, nw = w.shape
    in_specs = [pl.BlockSpec((tm, k), lambda i, j: (i, a_col)), pl.BlockSpec((k, tn), lambda i, j: (0, j))]
    in_specs += [s for _, s in extras]
    return pl.pallas_call(
        functools.partial(_mm_kernel, epilogue=epilogue, n_extra=len(extras)),
        out_shape=out_shape, grid=(m // tm, nw // tn), in_specs=in_specs, out_specs=out_spec,
        compiler_params=_params("arbitrary", "arbitrary"), name=name,
    )(a, w, *[x for x, _ in extras])


def _epi_rmsnorm(acc, extra, outs, *, eps):
    (g_ref,), (o_ref,) = extra, outs
    ms = jnp.mean(acc * acc, axis=-1, keepdims=True)
    o_ref[...] = (acc * lax.rsqrt(ms + eps) * g_ref[...]).astype(o_ref.dtype)


def _epi_scale(acc, extra, outs):
    (s_ref,), (o_ref,) = extra, outs
    o_ref[...] = (acc * s_ref[...]).astype(o_ref.dtype)


def _epi_plain(acc, extra, outs):
    outs[0][...] = acc.astype(outs[0].dtype)


def _epi_rope_tiles(acc, extra, outs, *, rope_tiles, norm, eps, scale):
    if norm:
        g_ref, cos_ref, sin_ref = extra
    else:
        cos_ref, sin_ref = extra
    o_ref = outs[0]
    cos, sin = cos_ref[...], sin_ref[...]
    for t in range(acc.shape[1] // LANE):
        v = acc[:, t * LANE:(t + 1) * LANE]
        if norm:
            ms = jnp.mean(v * v, axis=-1, keepdims=True)
            v = v * lax.rsqrt(ms + eps) * g_ref[...]
        if rope_tiles[t % len(rope_tiles)]:
            v = _rope(v, cos, sin)
        o_ref[:, t * LANE:(t + 1) * LANE] = (v * scale).astype(o_ref.dtype)


def _epi_mla_key(acc, extra, outs):
    (kpe_ref,), (o_ref,) = extra, outs
    o_ref[:, :LANE] = acc.astype(o_ref.dtype)
    o_ref[:, LANE:] = kpe_ref[...]


def _epi_residual(acc, extra, outs, *, tiles_per_batch, tm, n_ctx):
    (x_ref, tab_ref), (o_ref,) = extra, outs
    is_ctx = _ctx_mask(pl.program_id(0), tiles_per_batch, tm, n_ctx)
    tab = tab_ref[0]
    gate = jnp.where(is_ctx, tab[0:1], tab[1:2])
    o_ref[...] = x_ref[...] + gate * acc


def _attn_kernel(q_ref, k_ref, v_ref, o_ref, *, n_ctx, n_steps, tk, hps, kvps, dk, dv):
    tq = q_ref.shape[0]
    assert dv == LANE
    qs = [q_ref[:, i * dk:(i + 1) * dk] for i in range(hps)]

    def step(start, size, state):
        ones = jnp.ones((size, LANE), BF16)
        new = []
        for i, (m, acc) in enumerate(state):
            kv = i * kvps // hps
            k = k_ref[pl.ds(start, size), kv * dk:(kv + 1) * dk]
            v = jnp.concatenate([v_ref[pl.ds(start, size), kv * dv:(kv + 1) * dv], ones], axis=1)
            s = lax.dot_general(qs[i], k, (((1,), (1,)), ((), ())), preferred_element_type=F32)
            m_new = jnp.maximum(m, jnp.max(s, axis=-1, keepdims=True))
            p = jnp.exp2(s - m_new).astype(BF16)
            acc = jnp.exp2(m - m_new) * acc + jnp.dot(p, v, preferred_element_type=F32)
            new.append((m_new, acc))
        return tuple(new)

    state = tuple((jnp.full((tq, 1), NEG, F32), jnp.zeros((tq, dv + LANE), F32)) for _ in range(hps))
    state = step(0, n_ctx, state)

    def body(c, state):
        return step(pl.multiple_of(n_ctx + c * tk, n_ctx), tk, state)

    state = lax.fori_loop(0, n_steps, body, state)
    for i, (_, acc) in enumerate(state):
        o_ref[:, i * dv:(i + 1) * dv] = (acc[:, :dv] / acc[:, dv:]).astype(o_ref.dtype)


def _attention(q, k, v, out, *, batch, n_ctx, n_lat, heads, kv_heads, dk, dv, q_col, k_col, v_col,
               q_tiles, cfg, name):
    t_rows = n_ctx + n_lat
    tq = n_ctx
    tpb = t_rows // tq
    q_off = 1 if q_tiles == "lat" else 0
    n_q = tpb - 1 if q_tiles == "lat" else 1
    group = heads // kv_heads
    tk = _divisor_tile(n_lat, cfg.kv_tile, tq)
    n_steps = n_lat // tk if q_tiles == "lat" else 0
    k_rows = t_rows if q_tiles == "lat" else n_ctx
    kpb = t_rows // k_rows
    hps = cfg.attn_heads_per_step
    assert heads % hps == 0 and q_col % hps == 0
    if group == 1:
        kvps = hps
        assert k_col % hps == 0 and v_col % hps == 0

        def kv_block(col, h):
            return col // hps + h
    else:
        kvps = 1
        assert group % hps == 0

        def kv_block(col, h):
            return col + (h * hps) // group

    def kern(q_ref, k_ref, v_ref, _, o_ref):
        _attn_kernel(q_ref, k_ref, v_ref, o_ref, n_ctx=n_ctx, n_steps=n_steps, tk=tk, hps=hps, kvps=kvps,
                     dk=dk, dv=dv)

    return pl.pallas_call(
        kern, out_shape=jax.ShapeDtypeStruct(out.shape, out.dtype),
        grid=(batch, heads // hps, n_q),
        in_specs=[pl.BlockSpec((tq, hps * dk), lambda b, h, i: (b * tpb + i + q_off, q_col // hps + h)),
                  pl.BlockSpec((k_rows, kvps * dk), lambda b, h, i: (b * kpb, kv_block(k_col, h))),
                  pl.BlockSpec((k_rows, kvps * dv), lambda b, h, i: (b * kpb, kv_block(v_col, h))),
                  pl.BlockSpec(memory_space=pl.ANY)],
        out_specs=pl.BlockSpec((tq, hps * dv), lambda b, h, i: (b * tpb + i + q_off, h)),
        input_output_aliases={3: 0},
        compiler_params=_params("arbitrary", "arbitrary", "arbitrary"), name=name,
    )(q, k, v, out)


def _natten_geometry(rows, kh, group):
    union = group + kh - 1
    assert rows % group == 0 and rows >= union
    base = np.clip(group * np.arange(rows // group) - kh // 2, 0, rows - union)
    return union, base


def _natten_kernel(type_ref, q_ref, k_ref, v_ref, bias_ref, o_ref, *, n_ctx, grid_w, rows, kh, group):
    del type_ref
    union, _ = _natten_geometry(rows, kh, group)
    base = jnp.clip(group * pl.program_id(2) - kh // 2, 0, rows - union)
    start = pl.multiple_of(n_ctx + base * grid_w, grid_w)
    q = q_ref[...]
    dn = (((1,), (1,)), ((), ()))
    k_loc = k_ref[pl.ds(start, union * grid_w), :]
    v_loc = v_ref[pl.ds(start, union * grid_w), :]
    s_loc = lax.dot_general(q, k_loc, dn, preferred_element_type=F32) + bias_ref[0, 0]
    s_ctx = lax.dot_general(q, k_ref[pl.ds(0, n_ctx), :], dn, preferred_element_type=F32)
    m = jnp.maximum(jnp.max(s_loc, axis=-1, keepdims=True), jnp.max(s_ctx, axis=-1, keepdims=True))
    p_loc = jnp.exp2(s_loc - m)
    p_ctx = jnp.exp2(s_ctx - m)
    l = jnp.sum(p_loc, axis=-1, keepdims=True) + jnp.sum(p_ctx, axis=-1, keepdims=True)
    acc = jnp.dot(p_loc.astype(BF16), v_loc, preferred_element_type=F32)
    acc = acc + jnp.dot(p_ctx.astype(BF16), v_ref[pl.ds(0, n_ctx), :], preferred_element_type=F32)
    o_ref[...] = (acc / l).astype(o_ref.dtype)


def _natten_bias(rpb, rows, grid_w, kh, kw, group, scale):
    full_kh = (rpb.shape[1] + 1) // 2
    union, base = _natten_geometry(rows, kh, group)
    n_groups = rows // group
    r = group * np.arange(n_groups)[:, None] + np.arange(group)[None, :]
    r0 = np.clip(r - kh // 2, 0, rows - kh)
    key_row = base[:, None] + np.arange(union)[None, :]
    valid = (key_row[:, None, :] >= r0[:, :, None]) & (key_row[:, None, :] < r0[:, :, None] + kh)
    assert (valid.sum(-1) == kh).all()
    dr = np.where(valid, key_row[:, None, :] - r[:, :, None] + full_kh - 1, -1)
    patterns, type_of_group = np.unique(dr.reshape(n_groups, -1), axis=0, return_inverse=True)
    dr = patterns.reshape(-1, group, union)
    row_hot = (dr[..., None] == np.arange(2 * full_kh - 1)).astype(np.float32)
    col = np.arange(grid_w)
    col0 = np.clip(col - kw // 2, 0, grid_w - kw)
    inside = (col[None, :] >= col0[:, None]) & (col[None, :] < col0[:, None] + kw)
    dc = col[None, :] - col[:, None] + kw - 1
    col_hot = ((dc[..., None] == np.arange(2 * kw - 1)) & inside[..., None]).astype(np.float32)
    hi = lax.Precision.HIGHEST
    t = jnp.einsum("trus,hsv->htruv", jnp.asarray(row_hot), rpb.astype(F32), precision=hi)
    b = jnp.einsum("htruv,ckv->htrcuk", t, jnp.asarray(col_hot), precision=hi)
    mask = (dr >= 0)[None, :, :, None, :, None] & inside[None, None, None, :, None, :]
    b = jnp.where(jnp.asarray(mask), b * scale, NEG)
    b = b.reshape(rpb.shape[0], dr.shape[0], group * grid_w, union * grid_w)
    return b, jnp.asarray(type_of_group.reshape(-1), jnp.int32)


def _natten(qkv, out, bias, group_type, *, batch, n_ctx, n_lat, heads, hd, grid_w, kh, name):
    t_rows = n_ctx + n_lat
    rows = n_lat // grid_w
    assert n_ctx % grid_w == 0
    group = n_ctx // grid_w
    tpb = t_rows // n_ctx
    union, _ = _natten_geometry(rows, kh, group)

    def kern(type_ref, q_ref, k_ref, v_ref, bias_ref, _, o_ref):
        _natten_kernel(type_ref, q_ref, k_ref, v_ref, bias_ref, o_ref, n_ctx=n_ctx, grid_w=grid_w, rows=rows,
                       kh=kh, group=group)

    return pl.pallas_call(
        kern, out_shape=jax.ShapeDtypeStruct(out.shape, out.dtype),
        grid_spec=pltpu.PrefetchScalarGridSpec(
            num_scalar_prefetch=1, grid=(batch, heads, rows // group),
            in_specs=[pl.BlockSpec((n_ctx, hd), lambda b, h, g, ty: (b * tpb + 1 + g, h)),
                      pl.BlockSpec((t_rows, hd), lambda b, h, g, ty: (b, heads + h)),
                      pl.BlockSpec((t_rows, hd), lambda b, h, g, ty: (b, 2 * heads + h)),
                      pl.BlockSpec((1, 1, n_ctx, union * grid_w), lambda b, h, g, ty: (h, ty[g], 0, 0)),
                      pl.BlockSpec(memory_space=pl.ANY)],
            out_specs=pl.BlockSpec((n_ctx, hd), lambda b, h, g, ty: (b * tpb + 1 + g, h))),
        input_output_aliases={5: 0},
        compiler_params=_params("arbitrary", "arbitrary", "arbitrary"), name=name,
    )(group_type, qkv, qkv, qkv, bias, out)


def _merge_kernel(h_ref, oa_ref, ob_ref, oc_ref, wg_ref, wb_ref, o_ref):
    h = h_ref[...]
    acc = None
    for i, o in enumerate((oa_ref, ob_ref, oc_ref)):
        gate = jax.nn.sigmoid(jnp.dot(h, wg_ref[i], preferred_element_type=F32))
        term = gate * jnp.dot(o[...], wb_ref[i], preferred_element_type=F32)
        acc = term if acc is None else acc + term
    o_ref[...] = acc.astype(o_ref.dtype)


def _merge(h, o_a, o_b, o_c, wg, wb, *, tm, tn):
    n_rows, d = h.shape
    bw = o_a.shape[1]
    nb = wg.shape[0]
    return pl.pallas_call(
        _merge_kernel, out_shape=jax.ShapeDtypeStruct((n_rows, d), BF16),
        grid=(n_rows // tm, d // tn),
        in_specs=[pl.BlockSpec((tm, d), lambda i, j: (i, 0))]
        + [pl.BlockSpec((tm, bw), lambda i, j: (i, 0))] * 3
        + [pl.BlockSpec((nb, d, tn), lambda i, j: (0, 0, j)),
           pl.BlockSpec((nb, bw, tn), lambda i, j: (0, 0, j))],
        out_specs=pl.BlockSpec((tm, tn), lambda i, j: (i, j)),
        compiler_params=_params("arbitrary", "arbitrary"), name="gated_merge",
    )(h, o_a, o_b, o_c, wg, wb)


def _deinterleave_kernel(w_ref, sel_ref, o_ref):
    sel = sel_ref[...]
    for c in range(w_ref.shape[3] // (2 * LANE)):
        sl = slice(c * 2 * LANE, (c + 1) * 2 * LANE)
        blk = w_ref[0, 0, :, sl].astype(BF16)
        o_ref[0, :, sl] = jnp.dot(blk, sel, preferred_element_type=F32).astype(BF16)


def _deinterleave_up_weights(w1_all, layer):
    _, n_exp, d, f2 = w1_all.shape
    assert f2 % (2 * LANE) == 0
    tr = _divisor_tile(d, 256, 8)
    sel = np.zeros((2 * LANE, 2 * LANE), np.float32)
    sel[2 * np.arange(LANE), np.arange(LANE)] = 1.0
    sel[2 * np.arange(LANE) + 1, LANE + np.arange(LANE)] = 1.0
    return pl.pallas_call(
        _deinterleave_kernel, out_shape=jax.ShapeDtypeStruct((n_exp, d, f2), BF16),
        grid=(n_exp, d // tr),
        in_specs=[pl.BlockSpec((1, 1, tr, f2), lambda e, r: (layer, e, r, 0)),
                  pl.BlockSpec((2 * LANE, 2 * LANE), lambda e, r: (0, 0))],
        out_specs=pl.BlockSpec((1, tr, f2), lambda e, r: (e, r, 0)),
        compiler_params=_params("arbitrary", "arbitrary"), name="moe_weight_deinterleave",
    )(w1_all, jnp.asarray(sel, BF16))


def _dispatch_kernel(dest_ref, src_ref, _, dst_ref, sem, *, tm, top_k):
    base = pl.program_id(0) * tm

    def copies(t):
        for k in range(top_k):
            d = dest_ref[(base + t) * top_k + k]
            yield d, pltpu.make_async_copy(src_ref.at[t], dst_ref.at[jnp.maximum(d, 0)], sem)

    def issue(t, _):
        for d, cp in copies(t):
            pl.when(d >= 0)(cp.start)
        return 0

    def drain(t, _):
        for d, cp in copies(t):
            pl.when(d >= 0)(cp.wait)
        return 0

    lax.fori_loop(0, tm, issue, 0)
    lax.fori_loop(0, tm, drain, 0)


def _dispatch(dest, h3, xg0, *, tm, top_k):
    n_tok, s, lanes = h3.shape
    return pl.pallas_call(
        functools.partial(_dispatch_kernel, tm=tm, top_k=top_k),
        out_shape=jax.ShapeDtypeStruct(xg0.shape, xg0.dtype),
        grid_spec=pltpu.PrefetchScalarGridSpec(
            num_scalar_prefetch=1, grid=(n_tok // tm,),
            in_specs=[pl.BlockSpec((tm, s, lanes), lambda i, dest: (i, 0, 0)),
                      pl.BlockSpec(memory_space=pl.ANY)],
            out_specs=pl.BlockSpec(memory_space=pl.ANY),
            scratch_shapes=[pltpu.SemaphoreType.DMA]),
        input_output_aliases={2: 0},
        compiler_params=pltpu.CompilerParams(dimension_semantics=("arbitrary",),
                                             vmem_limit_bytes=VMEM_LIMIT, disable_bounds_checks=True),
        name="moe_dispatch",
    )(dest, h3, xg0)


def _swiglu(x_glu, x_lin, alpha, limit):
    x_glu = jnp.minimum(x_glu, limit)
    x_lin = jnp.clip(x_lin, -limit, limit)
    return x_glu * jax.nn.sigmoid(alpha * x_glu) * (x_lin + 1.0)


def _moe_up_kernel(be_ref, nu_ref, x_ref, w_ref, b_ref, a_ref, *, tc, alpha, limit):
    i = pl.program_id(0)

    @pl.when(i < nu_ref[0])
    def _():
        x = x_ref[...]
        for c in range(w_ref.shape[2] // tc):
            sl = slice(c * tc, (c + 1) * tc)
            hdn = jnp.dot(x, w_ref[0, :, sl], preferred_element_type=F32) + b_ref[0, :, sl]
            for j in range(tc // (2 * LANE)):
                g = hdn[:, 2 * j * LANE:(2 * j + 1) * LANE]
                lin = hdn[:, (2 * j + 1) * LANE:(2 * j + 2) * LANE]
                col = (c * tc // (2 * LANE) + j) * LANE
                a_ref[:, col:col + LANE] = _swiglu(g, lin, alpha, limit).astype(a_ref.dtype)

    @pl.when(i >= nu_ref[0])
    def _():
        a_ref[...] = jnp.zeros(a_ref.shape, a_ref.dtype)


def _moe_down_kernel(be_ref, nu_ref, a_ref, w_ref, b_ref, y_ref):
    i = pl.program_id(0)

    @pl.when(i < nu_ref[0])
    def _():
        y_ref[...] = jnp.dot(a_ref[...], w_ref[0], preferred_element_type=F32) + b_ref[0]

    @pl.when(i >= nu_ref[0])
    def _():
        y_ref[...] = jnp.zeros(y_ref.shape, y_ref.dtype)


def _moe_experts(blk_e, n_used, xg, w1, b1, w2, b2, *, bm, cfg):
    p_rows, d = xg.shape
    n_exp, _, f2 = w1.shape
    ff = f2 // 2
    nb = p_rows // bm
    tc = _divisor_tile(f2, 1024, 2 * LANE)
    a = pl.pallas_call(
        functools.partial(_moe_up_kernel, tc=tc, alpha=cfg.swiglu_alpha, limit=cfg.swiglu_limit),
        out_shape=jax.ShapeDtypeStruct((p_rows, ff), BF16),
        grid_spec=pltpu.PrefetchScalarGridSpec(
            num_scalar_prefetch=2, grid=(nb,),
            in_specs=[pl.BlockSpec((bm, d), lambda i, be, nu: (i, 0)),
                      pl.BlockSpec((1, d, f2), lambda i, be, nu: (be[i], 0, 0)),
                      pl.BlockSpec((1, 1, f2), lambda i, be, nu: (be[i], 0, 0))],
            out_specs=pl.BlockSpec((bm, ff), lambda i, be, nu: (i, 0))),
        compiler_params=_params("arbitrary"), name="moe_up_swiglu",
    )(blk_e, n_used, xg, w1, b1)
    return pl.pallas_call(
        _moe_down_kernel,
        out_shape=jax.ShapeDtypeStruct((p_rows, d), F32),
        grid_spec=pltpu.PrefetchScalarGridSpec(
            num_scalar_prefetch=2, grid=(nb,),
            in_specs=[pl.BlockSpec((bm, ff), lambda i, be, nu: (i, 0)),
                      pl.BlockSpec((1, ff, d), lambda i, be, nu: (be[i], 0, 0)),
                      pl.BlockSpec((1, 1, d), lambda i, be, nu: (be[i], 0, 0))],
            out_specs=pl.BlockSpec((bm, d), lambda i, be, nu: (i, 0))),
        compiler_params=_params("arbitrary"), name="moe_down",
    )(blk_e, n_used, a, w2, b2)


def _combine_kernel(dest_ref, y_ref, w_ref, o_ref, buf, sem, *, tm, top_k):
    base = pl.program_id(0) * tm

    def copies(t):
        for k in range(top_k):
            yield pltpu.make_async_copy(y_ref.at[dest_ref[(base + t) * top_k + k]], buf.at[k, t], sem)

    def issue(t, _):
        for cp in copies(t):
            cp.start()
        return 0

    def drain(t, _):
        for cp in copies(t):
            cp.wait()
        return 0

    lax.fori_loop(0, tm, issue, 0)
    lax.fori_loop(0, tm, drain, 0)
    acc = buf[0] * w_ref[0]
    for k in range(1, top_k):
        acc = acc + buf[k] * w_ref[k]
    o_ref[...] = acc


def _combine(dest, y3, w_rep, *, tm, top_k):
    _, s, lanes = y3.shape
    n_tok = w_rep.shape[1]
    return pl.pallas_call(
        functools.partial(_combine_kernel, tm=tm, top_k=top_k),
        out_shape=jax.ShapeDtypeStruct((n_tok, s, lanes), F32),
        grid_spec=pltpu.PrefetchScalarGridSpec(
            num_scalar_prefetch=1, grid=(n_tok // tm,),
            in_specs=[pl.BlockSpec(memory_space=pl.ANY),
                      pl.BlockSpec((top_k, tm, 1, lanes), lambda i, dest: (0, i, 0, 0))],
            out_specs=pl.BlockSpec((tm, s, lanes), lambda i, dest: (i, 0, 0)),
            scratch_shapes=[pltpu.VMEM((top_k, tm, s, lanes), F32), pltpu.SemaphoreType.DMA]),
        compiler_params=pltpu.CompilerParams(dimension_semantics=("arbitrary",),
                                             vmem_limit_bytes=VMEM_LIMIT, disable_bounds_checks=True),
        name="moe_combine",
    )(dest, y3, w_rep)


def _moe(h, top_i, top_w, valid, w1, b1, w2, b2, *, cfg):
    n_tok, d = h.shape
    n_exp = w1.shape[0]
    kk = cfg.top_k
    bm = cfg.moe_rows
    idx = top_i[:, :kk]
    sel = (idx[:, :, None] == jnp.arange(n_exp, dtype=jnp.int32)[None, None, :]) & valid[:, None, None]
    sel = jnp.any(sel, axis=1).astype(jnp.int32)
    counts = jnp.sum(sel, axis=0)
    rank = jnp.cumsum(sel, axis=0) - sel
    padded = (counts + bm - 1) // bm * bm
    gend = jnp.cumsum(padded)
    gstart = gend - padded
    onehot = (idx[:, :, None] == jnp.arange(n_exp, dtype=jnp.int32)[None, None, :]).astype(jnp.int32)
    dest = jnp.sum(onehot * (gstart[None, None, :] + rank[:, None, :]), axis=-1)
    n_blocks = -(-(n_tok * kk) // bm) + n_exp
    p_rows = n_blocks * bm
    starts = jnp.arange(n_blocks, dtype=jnp.int32) * bm
    blk_e = jnp.minimum(jnp.sum(gend[None, :] <= starts[:, None], axis=1), n_exp - 1).astype(jnp.int32)
    n_used = (gend[-1] // bm).astype(jnp.int32).reshape(1)

    s = d // LANE
    dest_w = jnp.where(valid[:, None], dest, -1).astype(jnp.int32).reshape(-1)
    tmd = _divisor_tile(n_tok, cfg.gather_rows, 16)
    xg = _dispatch(dest_w, h.reshape(n_tok, s, LANE), jnp.zeros((p_rows, s, LANE), h.dtype), tm=tmd, top_k=kk)
    y = _moe_experts(blk_e, n_used, xg.reshape(p_rows, d), w1, b1, w2, b2, bm=bm, cfg=cfg)
    dest_r = jnp.where(valid[:, None], dest, 0).astype(jnp.int32).reshape(-1)
    w_rep = jnp.broadcast_to(top_w[:, :kk].T[:, :, None, None], (kk, n_tok, 1, LANE))
    tmc = _divisor_tile(n_tok, cfg.combine_rows, 8)
    f = _combine(dest_r, y.reshape(p_rows, s, LANE), w_rep, tm=tmc, top_k=kk)
    return f.reshape(n_tok, d)


def _final_kernel(x_ref, f_ref, gate_ref, g_ref, o_ref, *, eps):
    x = x_ref[...] + gate_ref[0] * f_ref[...]
    ms = jnp.mean(x * x, axis=-1, keepdims=True)
    o_ref[0] = x * lax.rsqrt(ms + eps) * g_ref[...]


def _final_norm(x, f, gate, g, *, batch, n_ctx, n_lat, eps):
    d = x.shape[1]
    ts = n_ctx
    tpb = (n_ctx + n_lat) // ts
    row = pl.BlockSpec((ts, d), lambda b, s: (b * tpb + 1 + s, 0))
    return pl.pallas_call(
        functools.partial(_final_kernel, eps=eps),
        out_shape=jax.ShapeDtypeStruct((batch, n_lat, d), F32),
        grid=(batch, n_lat // ts),
        in_specs=[row, row, pl.BlockSpec((1, 1, d), lambda b, s: (b, 0, 0)),
                  pl.BlockSpec((1, d), lambda b, s: (0, 0))],
        out_specs=pl.BlockSpec((1, ts, d), lambda b, s: (b, s, 0)),
        compiler_params=_params("arbitrary", "arbitrary"), name="final_norm",
    )(x, f, gate, g.reshape(1, d))


def _rope_tables(kind, n_ctx, n_lat, grid_w, theta, dims):
    half = dims // 2
    nf = half // 2
    freqs = theta ** (-np.arange(0, half, 2, dtype=np.float64) / half)
    t = np.arange(n_lat)
    pos_r = np.concatenate([np.zeros(n_ctx), t // grid_w]).astype(np.float64)
    pos_c = np.concatenate([np.zeros(n_ctx), t % grid_w]).astype(np.float64)
    ang_r = pos_r[:, None] * freqs[None, :]
    ang_c = pos_c[:, None] * freqs[None, :]
    ang = np.concatenate([ang_r, ang_c], axis=1)
    rows = n_ctx + n_lat
    cos = np.ones((rows, LANE), np.float32)
    sin = np.zeros((rows, LANE), np.float32)
    w = 2 * nf
    cos[:, :w] = np.cos(ang)
    cos[:, LANE // 2:LANE // 2 + w] = np.cos(ang)
    sin[:, :w] = -np.sin(ang)
    sin[:, LANE // 2:LANE // 2 + w] = np.sin(ang)
    del kind
    return jnp.asarray(cos), jnp.asarray(sin)


def _axial_lanes(w, dims):
    nf = dims // 4
    lead = w.shape[:-1]
    w = jnp.swapaxes(w.reshape(lead + (2, 2, nf)), -3, -2).reshape(lead + (2, 2 * nf))
    w = jnp.pad(w, [(0, 0)] * (len(lead) + 1) + [(0, LANE // 2 - 2 * nf)])
    return w.reshape(lead + (LANE,))


def _forward(x, c, ctx, c_ctx, w_mod, b_mod, g_mix, w_in, g_q_a, w_uq, g_kv_a, w_ukv, g_qn, g_kn,
             rpb, w_branch, w_out, g_ffn, w_router, b_router, w_exp1, b_exp1, w_exp2, b_exp2, g_final,
             cfg):
    batch, n_lat, d = x.shape
    n_ctx = ctx.shape[1]
    depth = w_mod.shape[0]
    t_rows = n_ctx + n_lat
    n_rows = batch * t_rows
    hd = cfg.head_dim
    q_rank, kv_rank = g_q_a.shape[1], g_kv_a.shape[1]
    ha, hb, hkv, hc = cfg.mla_heads, cfg.gqa_heads, cfg.gqa_kv_heads, cfg.na_heads
    nope, rdim, vdim = cfg.mla_nope, cfg.mla_rope, cfg.mla_v
    assert nope == LANE and vdim == LANE and hd == LANE and rdim <= LANE // 2
    assert n_lat % n_ctx == 0 and n_ctx % 16 == 0 and batch <= 7
    kh_full = (rpb.shape[2] + 1) // 2
    kw = (rpb.shape[3] + 1) // 2
    rows = n_lat // cfg.grid_w
    kh = min(kh_full, rows)
    n_branch = w_branch.shape[1]
    assert n_branch == 3

    tm = _divisor_tile(t_rows, cfg.row_tile, 16)
    tpb = t_rows // tm

    sizes = (q_rank, kv_rank, rdim, hb * hd, hkv * hd, hkv * hd, hc * hd, hc * hd, hc * hd, n_branch * d)
    offs = np.concatenate([[0], np.cumsum(sizes)])

    cos_a, sin_a = _rope_tables("mla", n_ctx, n_lat, cfg.grid_w, cfg.rope_theta, rdim)
    cos_b, sin_b = _rope_tables("gqa", n_ctx, n_lat, cfg.grid_w, cfg.rope_theta, hd)
    rope_spec = pl.BlockSpec((tm, LANE), lambda i, j: (i % tpb, 0))
    log2e = float(np.log2(np.e))
    mla_scale = float((nope + rdim) ** -0.5) * log2e
    head_scale = float(hd ** -0.5) * log2e

    cc = jnp.zeros((8, d), F32).at[:batch].set(c).at[batch].set(c_ctx)
    mod = _modulation(cc, w_mod, b_mod)

    def table(layer, chunks):
        rows_ = []
        for ch in chunks:
            sl = mod[layer, :, ch * d:(ch + 1) * d]
            rows_.append(jnp.broadcast_to(sl[batch][None], (batch, d)))
            rows_.append(sl[:batch])
        tab = jnp.stack(rows_, axis=1)
        return jnp.pad(tab, ((0, 0), (0, 8 - tab.shape[1]), (0, 0)))

    xa = jnp.concatenate([ctx, x], axis=1).reshape(n_rows, d)
    pend_f = None
    pend_tab = None
    is_lat = (jnp.arange(n_rows, dtype=jnp.int32) % t_rows) >= n_ctx

    for layer in range(depth):
        last = layer == depth - 1
        wi = w_in[layer]

        w_d = wi[:, offs[0]:offs[2]].astype(BF16)
        g_d = jnp.concatenate([g_q_a[layer], g_kv_a[layer]]).reshape(1, -1)
        w_kr = _axial_lanes(wi[:, offs[2]:offs[3]], rdim).astype(BF16)
        uq = w_uq[layer].reshape(q_rank, ha, nope + rdim)
        uq = jnp.concatenate([uq[:, :, :nope], _axial_lanes(uq[:, :, nope:], rdim)], axis=-1)
        uq = uq.reshape(q_rank, ha * 2 * LANE).astype(BF16)
        ukv = w_ukv[layer].reshape(kv_rank, ha, nope + vdim)
        uk = ukv[:, :, :nope].reshape(kv_rank, ha * nope).astype(BF16)
        uv = ukv[:, :, nope:].reshape(kv_rank, ha * vdim).astype(BF16)
        w_qb = _axial_lanes(wi[:, offs[3]:offs[4]].reshape(d, hb, hd), hd).reshape(d, hb * hd).astype(BF16)
        w_kb = _axial_lanes(wi[:, offs[4]:offs[5]].reshape(d, hkv, hd), hd).reshape(d, hkv * hd).astype(BF16)
        w_vb = wi[:, offs[5]:offs[6]].astype(BF16)
        g_qn_p = _axial_lanes(g_qn[layer], hd).reshape(1, hd)
        g_kn_p = _axial_lanes(g_kn[layer], hd).reshape(1, hd)
        w_c = wi[:, offs[6]:offs[9]].astype(BF16)
        c_scale = jnp.concatenate([jnp.full((hc * hd,), head_scale, F32),
                                   jnp.ones((2 * hc * hd,), F32)]).reshape(1, -1)
        w_g = wi[:, offs[9]:offs[10]].reshape(d, n_branch, d).transpose(1, 0, 2).astype(BF16)
        w_b = w_branch[layer].astype(BF16)
        w_o = w_out[layer].astype(BF16)
        ff = w_exp2.shape[2]
        w1 = _deinterleave_up_weights(w_exp1, layer)
        b1 = b_exp1[layer].reshape(-1, ff // LANE, LANE, 2).transpose(0, 1, 3, 2).reshape(-1, 1, 2 * ff)
        w2 = w_exp2[layer].astype(BF16)
        b2 = b_exp2[layer].reshape(-1, 1, d)

        tab1 = table(layer, (0, 1)) if pend_f is None else jnp.concatenate(
            [table(layer, (0, 1))[:, :4], pend_tab[:, :2], jnp.zeros((batch, 2, d), F32)], axis=1)
        res = _norm_mod(xa, pend_f, tab1, g_mix[layer], None, t_rows=t_rows, n_ctx=n_ctx, cfg=cfg)
        if pend_f is not None:
            xa, h = res
        else:
            (h,) = res

        dn = _matmul(h, w_d, functools.partial(_epi_rmsnorm, eps=cfg.eps),
                     [(g_d, pl.BlockSpec((1, q_rank), lambda i, j: (0, j)))],
                     jax.ShapeDtypeStruct((n_rows, q_rank + kv_rank), BF16),
                     pl.BlockSpec((tm, q_rank), lambda i, j: (i, j)), tm=tm, tn=q_rank, name="mla_down")
        assert q_rank == kv_rank
        k_pe = _matmul(h, w_kr,
                       functools.partial(_epi_rope_tiles, rope_tiles=(True,), norm=False, eps=cfg.eps, scale=1.0),
                       [(cos_a, rope_spec), (sin_a, rope_spec)],
                       jax.ShapeDtypeStruct((n_rows, LANE), BF16),
                       pl.BlockSpec((tm, LANE), lambda i, j: (i, 0)), tm=tm, tn=LANE, name="mla_rope_key")
        q_a = _matmul(dn, uq,
                      functools.partial(_epi_rope_tiles, rope_tiles=(False, True), norm=False, eps=cfg.eps,
                                        scale=mla_scale),
                      [(cos_a, rope_spec), (sin_a, rope_spec)],
                      jax.ShapeDtypeStruct((n_rows, ha * 2 * LANE), BF16),
                      pl.BlockSpec((tm, 2 * LANE), lambda i, j: (i, j)), tm=tm, tn=2 * LANE, a_col=0,
                      name="mla_q_up")
        k_a = _matmul(dn, uk, _epi_mla_key,
                      [(k_pe, pl.BlockSpec((tm, LANE), lambda i, j: (i, 0)))],
                      jax.ShapeDtypeStruct((n_rows, ha * 2 * LANE), BF16),
                      pl.BlockSpec((tm, 2 * LANE), lambda i, j: (i, j)), tm=tm, tn=LANE, a_col=1,
                      name="mla_k_up")
        tn_v = _divisor_tile(ha * vdim, 4 * LANE, LANE)
        v_a = _matmul(dn, uv, _epi_plain, [],
                      jax.ShapeDtypeStruct((n_rows, ha * vdim), BF16),
                      pl.BlockSpec((tm, tn_v), lambda i, j: (i, j)), tm=tm, tn=tn_v, a_col=1, name="mla_v_up")

        tn_q = _divisor_tile(hb * hd, 4 * LANE, LANE)
        q_b = _matmul(h, w_qb,
                      functools.partial(_epi_rope_tiles, rope_tiles=(True,), norm=True, eps=cfg.eps,
                                        scale=head_scale),
                      [(g_qn_p, pl.BlockSpec((1, hd), lambda i, j: (0, 0))), (cos_b, rope_spec), (sin_b, rope_spec)],
                      jax.ShapeDtypeStruct((n_rows, hb * hd), BF16),
                      pl.BlockSpec((tm, tn_q), lambda i, j: (i, j)), tm=tm, tn=tn_q, name="gqa_q")
        tn_k = _divisor_tile(hkv * hd, 4 * LANE, LANE)
        k_b = _matmul(h, w_kb,
                      functools.partial(_epi_rope_tiles, rope_tiles=(True,), norm=True, eps=cfg.eps, scale=1.0),
                      [(g_kn_p, pl.BlockSpec((1, hd), lambda i, j: (0, 0))), (cos_b, rope_spec), (sin_b, rope_spec)],
                      jax.ShapeDtypeStruct((n_rows, hkv * hd), BF16),
                      pl.BlockSpec((tm, tn_k), lambda i, j: (i, j)), tm=tm, tn=tn_k, name="gqa_k")
        v_b = _matmul(h, w_vb, _epi_plain, [],
                      jax.ShapeDtypeStruct((n_rows, hkv * hd), BF16),
                      pl.BlockSpec((tm, tn_k), lambda i, j: (i, j)), tm=tm, tn=tn_k, name="gqa_v")

        tn_c = _divisor_tile(3 * hc * hd, 4 * LANE, LANE)
        qkv_c = _matmul(h, w_c, _epi_scale, [(c_scale, pl.BlockSpec((1, tn_c), lambda i, j: (0, j)))],
                        jax.ShapeDtypeStruct((n_rows, 3 * hc * hd), BF16),
                        pl.BlockSpec((tm, tn_c), lambda i, j: (i, j)), tm=tm, tn=tn_c, name="natten_qkv")

        common = dict(batch=batch, n_ctx=n_ctx, n_lat=n_lat, cfg=cfg)
        assert ha * vdim == hb * hd == hc * hd
        o_a = o_b = o_c = jnp.zeros((n_rows, hc * hd), BF16)
        spec_a = dict(heads=ha, kv_heads=ha, dk=2 * LANE, dv=vdim, q_col=0, k_col=0, v_col=0)
        spec_b = dict(heads=hb, kv_heads=hkv, dk=hd, dv=hd, q_col=0, k_col=0, v_col=0)
        if not last:
            o_a = _attention(q_a, k_a, v_a, o_a, q_tiles="ctx", name="mla_ctx_attention", **spec_a, **common)
            o_b = _attention(q_b, k_b, v_b, o_b, q_tiles="ctx", name="gqa_ctx_attention", **spec_b, **common)
            o_c = _attention(qkv_c, qkv_c, qkv_c, o_c, heads=hc, kv_heads=hc, dk=hd, dv=hd,
                             q_col=0, k_col=hc, v_col=2 * hc, q_tiles="ctx", name="natten_ctx_attention",
                             **common)
        o_a = _attention(q_a, k_a, v_a, o_a, q_tiles="lat", name="mla_attention", **spec_a, **common)
        o_b = _attention(q_b, k_b, v_b, o_b, q_tiles="lat", name="gqa_attention", **spec_b, **common)
        bias, group_type = _natten_bias(rpb[layer], rows, cfg.grid_w, kh, kw, n_ctx // cfg.grid_w, log2e)
        o_c = _natten(qkv_c, o_c, bias, group_type, batch=batch, n_ctx=n_ctx, n_lat=n_lat, heads=hc, hd=hd,
                      grid_w=cfg.grid_w, kh=kh, name="natten")

        tn_m = _divisor_tile(d, 4 * LANE, LANE)
        merged = _merge(h, o_a, o_b, o_c, w_g, w_b, tm=tm, tn=tn_m)
        gate1 = table(layer, (2,))
        x1 = _matmul(merged, w_o, functools.partial(_epi_residual, tiles_per_batch=tpb, tm=tm, n_ctx=n_ctx),
                     [(xa, pl.BlockSpec((tm, tn_m), lambda i, j: (i, j))),
                      (gate1, pl.BlockSpec((1, 8, tn_m), lambda i, j: (i // tpb, 0, j)))],
                     jax.ShapeDtypeStruct((n_rows, d), F32),
                     pl.BlockSpec((tm, tn_m), lambda i, j: (i, j)), tm=tm, tn=tn_m, name="out_proj_residual")

        h2, top_i, top_w = _norm_mod(x1, None, table(layer, (3, 4)), g_ffn[layer],
                                     (w_router[layer], b_router[layer]), t_rows=t_rows, n_ctx=n_ctx, cfg=cfg)
        valid = is_lat if last else jnp.ones((n_rows,), bool)
        pend_f = _moe(h2, top_i, top_w, valid, w1, b1, w2, b2, cfg=cfg)
        pend_tab = table(layer, (5,))
        xa = x1

    gate2 = pend_tab[:, 1:2]
    return _final_norm(xa, pend_f, gate2, g_final, batch=batch, n_ctx=n_ctx, n_lat=n_lat, eps=cfg.eps)


def kernel(x, c, ctx, c_ctx, w_mod, b_mod, g_mix, w_in, g_q_a, w_uq, g_kv_a, w_ukv, g_qn, g_kn, rpb,
           w_branch, w_out, g_ffn, w_router, b_router, w_exp1, b_exp1, w_exp2, b_exp2, g_final):
    return _forward(x, c, ctx, c_ctx, w_mod, b_mod, g_mix, w_in, g_q_a, w_uq, g_kv_a, w_ukv, g_qn, g_kn,
                    rpb, w_branch, w_out, g_ffn, w_router, b_router, w_exp1, b_exp1, w_exp2, b_exp2,
                    g_final, CFG)
```

```python
import functools
from typing import NamedTuple

import numpy as np
import jax
import jax.numpy as jnp
from jax import lax
from jax.experimental import pallas as pl
from jax.experimental.pallas import tpu as pltpu

F32 = jnp.float32
BF16 = jnp.bfloat16

LANE = 128
VMEM_LIMIT = 56 * 1024 * 1024
NEG = -1e30


class Config(NamedTuple):
    grid_w: int = 64
    eps: float = 1e-6
    rope_theta: float = 10000.0
    mla_heads: int = 8
    mla_nope: int = 128
    mla_rope: int = 64
    mla_v: int = 128
    gqa_heads: int = 8
    gqa_kv_heads: int = 2
    head_dim: int = 128
    na_heads: int = 8
    top_k: int = 4
    swiglu_alpha: float = 1.702
    swiglu_limit: float = 7.0
    row_tile: int = 640
    kv_tile: int = 2048
    attn_heads_per_step: int = 4
    moe_rows: int = 256
    gather_rows: int = 512
    combine_rows: int = 128


CFG = Config()


def _divisor_tile(n, target, mult):
    best = None
    for d in range(mult, min(n, target) + 1, mult):
        if n % d == 0:
            best = d
    assert best is not None, (n, target, mult)
    return best


def _params(*sem):
    return pltpu.CompilerParams(dimension_semantics=sem, vmem_limit_bytes=VMEM_LIMIT)


def _mod_kernel(c_ref, w_ref, b_ref, o_ref):
    c = c_ref[...]
    a = (c * jax.nn.sigmoid(c)).astype(BF16)
    o_ref[0] = jnp.dot(a, w_ref[0].astype(BF16), preferred_element_type=F32) + b_ref[0]


def _modulation(cc, w_mod, b_mod):
    depth, d, n6 = w_mod.shape
    tn = _divisor_tile(n6, 1024, LANE)
    return pl.pallas_call(
        _mod_kernel,
        out_shape=jax.ShapeDtypeStruct((depth, 8, n6), F32),
        grid=(depth, n6 // tn),
        in_specs=[pl.BlockSpec((8, d), lambda l, j: (0, 0)),
                  pl.BlockSpec((1, d, tn), lambda l, j: (l, 0, j)),
                  pl.BlockSpec((1, 1, tn), lambda l, j: (l, 0, j))],
        out_specs=pl.BlockSpec((1, 8, tn), lambda l, j: (l, 0, j)),
        compiler_params=_params("arbitrary", "arbitrary"),
        name="adaln_modulation",
    )(cc, w_mod, b_mod.reshape(depth, 1, n6))


def _ctx_mask(tile_idx, tiles_per_batch, tm, n_ctx):
    row = (tile_idx % tiles_per_batch) * tm + lax.broadcasted_iota(jnp.int32, (tm, 1), 0)
    return row < n_ctx


def _norm_kernel(*refs, has_res, has_router, tiles_per_batch, tm, n_ctx, eps, top_k):
    it = iter(refs)
    x_ref = next(it)
    f_ref = next(it) if has_res else None
    tab_ref = next(it)
    g_ref = next(it)
    if has_router:
        wr_ref = next(it)
        br_ref = next(it)
    xo_ref = next(it) if has_res else None
    h_ref = next(it)
    if has_router:
        ti_ref = next(it)
        tw_ref = next(it)

    is_ctx = _ctx_mask(pl.program_id(0), tiles_per_batch, tm, n_ctx)
    tab = tab_ref[0]

    def pick(k):
        return jnp.where(is_ctx, tab[2 * k:2 * k + 1], tab[2 * k + 1:2 * k + 2])

    x = x_ref[...]
    if has_res:
        x = x + pick(2) * f_ref[...]
        xo_ref[...] = x
    ms = jnp.mean(x * x, axis=-1, keepdims=True)
    y = x * lax.rsqrt(ms + eps) * g_ref[...]
    n = y * (1.0 + pick(1)) + pick(0)
    h_ref[...] = n.astype(BF16)

    if has_router:
        logits = jnp.dot(n, wr_ref[...], preferred_element_type=F32,
                         precision=lax.Precision.HIGHEST) + br_ref[...]
        lane = lax.broadcasted_iota(jnp.int32, logits.shape, 1)
        vals, idxs = [], []
        cur = logits
        for _ in range(top_k):
            m = jnp.max(cur, axis=-1, keepdims=True)
            idx = jnp.min(jnp.where(cur == m, lane, LANE), axis=-1, keepdims=True)
            vals.append(m)
            idxs.append(idx)
            cur = jnp.where(lane == idx, NEG, cur)
        es = [jnp.exp(v - vals[0]) for v in vals]
        den = es[0]
        for e in es[1:]:
            den = den + e
        ti = jnp.zeros(logits.shape, jnp.int32)
        tw = jnp.zeros(logits.shape, F32)
        for k in range(top_k):
            ti = jnp.where(lane == k, idxs[k], ti)
            tw = jnp.where(lane == k, es[k] / den, tw)
        ti_ref[...] = ti
        tw_ref[...] = tw


def _norm_mod(x, f, tab, g, router, *, t_rows, n_ctx, cfg):
    n_rows, d = x.shape
    tm = _divisor_tile(t_rows, 256, 8)
    tpb = t_rows // tm
    has_res = f is not None
    has_router = router is not None
    row = pl.BlockSpec((tm, d), lambda i: (i, 0))
    ins = [x] + ([f] if has_res else []) + [tab, g.reshape(1, d)]
    in_specs = [row] + ([row] if has_res else []) + [
        pl.BlockSpec((1, 8, d), lambda i: (i // tpb, 0, 0)),
        pl.BlockSpec((1, d), lambda i: (0, 0))]
    outs, out_specs = [], []
    if has_res:
        outs.append(jax.ShapeDtypeStruct((n_rows, d), F32))
        out_specs.append(row)
    outs.append(jax.ShapeDtypeStruct((n_rows, d), BF16))
    out_specs.append(row)
    if has_router:
        w_r, b_r = router
        n_exp = w_r.shape[1]
        assert n_exp <= LANE
        wr = jnp.zeros((d, LANE), F32).at[:, :n_exp].set(w_r)
        br = jnp.full((1, LANE), NEG, F32).at[0, :n_exp].set(b_r)
        ins += [wr, br]
        in_specs += [pl.BlockSpec((d, LANE), lambda i: (0, 0)), pl.BlockSpec((1, LANE), lambda i: (0, 0))]
        lane_row = pl.BlockSpec((tm, LANE), lambda i: (i, 0))
        outs += [jax.ShapeDtypeStruct((n_rows, LANE), jnp.int32), jax.ShapeDtypeStruct((n_rows, LANE), F32)]
        out_specs += [lane_row, lane_row]
    res = pl.pallas_call(
        functools.partial(_norm_kernel, has_res=has_res, has_router=has_router, tiles_per_batch=tpb,
                          tm=tm, n_ctx=n_ctx, eps=cfg.eps, top_k=cfg.top_k),
        out_shape=outs, grid=(n_rows // tm,), in_specs=in_specs, out_specs=out_specs,
        compiler_params=_params("arbitrary"),
        name="norm_modulate_router" if has_router else "norm_modulate",
    )(*ins)
    return list(res)


def _rope(t, cos, sin):
    return t * cos + pltpu.roll(t, LANE // 2, 1) * sin


def _mm_kernel(*refs, epilogue, n_extra):
    a_ref, w_ref = refs[0], refs[1]
    extra = refs[2:2 + n_extra]
    outs = refs[2 + n_extra:]
    acc = jnp.dot(a_ref[...], w_ref[...], preferred_element_type=F32)
    epilogue(acc, extra, outs)


def _matmul(a, w, epilogue, extras, out_shape, out_spec, *, tm, tn, a_col=0, name):
    m = a.shape[0]
    k, nw = w.shape
    in_specs = [pl.BlockSpec((tm, k), lambda i, j: (i, a_col)), pl.BlockSpec((k, tn), lambda i, j: (0, j))]
    in_specs += [s for _, s in extras]
    return pl.pallas_call(
        functools.partial(_mm_kernel, epilogue=epilogue, n_extra=len(extras)),
        out_shape=out_shape, grid=(m // tm, nw // tn), in_specs=in_specs, out_specs=out_spec,
        compiler_params=_params("arbitrary", "arbitrary"), name=name,
    )(a, w, *[x for x, _ in extras])


def _epi_rmsnorm(acc, extra, outs, *, eps):
    (g_ref,), (o_ref,) = extra, outs
    ms = jnp.mean(acc * acc, axis=-1, keepdims=True)
    o_ref[...] = (acc * lax.rsqrt(ms + eps) * g_ref[...]).astype(o_ref.dtype)


def _epi_scale(acc, extra, outs):
    (s_ref,), (o_ref,) = extra, outs
    o_ref[...] = (acc * s_ref[...]).astype(o_ref.dtype)


def _epi_plain(acc, extra, outs):
    outs[0][...] = acc.astype(outs[0].dtype)


def _epi_rope_tiles(acc, extra, outs, *, rope_tiles, norm, eps, scale):
    if norm:
        g_ref, cos_ref, sin_ref = extra
    else:
        cos_ref, sin_ref = extra
    o_ref = outs[0]
    cos, sin = cos_ref[...], sin_ref[...]
    for t in range(acc.shape[1] // LANE):
        v = acc[:, t * LANE:(t + 1) * LANE]
        if norm:
            ms = jnp.mean(v * v, axis=-1, keepdims=True)
            v = v * lax.rsqrt(ms + eps) * g_ref[...]
        if rope_tiles[t % len(rope_tiles)]:
            v = _rope(v, cos, sin)
        o_ref[:, t * LANE:(t + 1) * LANE] = (v * scale).astype(o_ref.dtype)


def _epi_mla_key(acc, extra, outs):
    (kpe_ref,), (o_ref,) = extra, outs
    for hh in range(acc.shape[1] // LANE):
        o_ref[:, 2 * hh * LANE:(2 * hh + 1) * LANE] = acc[:, hh * LANE:(hh + 1) * LANE].astype(o_ref.dtype)
        o_ref[:, (2 * hh + 1) * LANE:(2 * hh + 2) * LANE] = kpe_ref[...]


def _epi_residual(acc, extra, outs, *, tiles_per_batch, tm, n_ctx):
    (x_ref, tab_ref), (o_ref,) = extra, outs
    is_ctx = _ctx_mask(pl.program_id(0), tiles_per_batch, tm, n_ctx)
    tab = tab_ref[0]
    gate = jnp.where(is_ctx, tab[0:1], tab[1:2])
    o_ref[...] = x_ref[...] + gate * acc


def _attn_kernel(q_ref, k_ref, v_ref, o_ref, *, n_ctx, n_steps, tk, hps, kvps, dk, dv):
    tq = q_ref.shape[0]
    assert dv == LANE
    qs = [q_ref[:, i * dk:(i + 1) * dk] for i in range(hps)]

    def step(start, size, state):
        ones = jnp.ones((size, LANE), BF16)
        new = []
        for i, (m, acc) in enumerate(state):
            kv = i * kvps // hps
            k = k_ref[pl.ds(start, size), kv * dk:(kv + 1) * dk]
            v = jnp.concatenate([v_ref[pl.ds(start, size), kv * dv:(kv + 1) * dv], ones], axis=1)
            s = lax.dot_general(qs[i], k, (((1,), (1,)), ((), ())), preferred_element_type=F32)
            m_new = jnp.maximum(m, jnp.max(s, axis=-1, keepdims=True))
            p = jnp.exp2(s - m_new).astype(BF16)
            acc = jnp.exp2(m - m_new) * acc + jnp.dot(p, v, preferred_element_type=F32)
            new.append((m_new, acc))
        return tuple(new)

    state = tuple((jnp.full((tq, 1), NEG, F32), jnp.zeros((tq, dv + LANE), F32)) for _ in range(hps))
    state = step(0, n_ctx, state)

    def body(c, state):
        return step(pl.multiple_of(n_ctx + c * tk, n_ctx), tk, state)

    state = lax.fori_loop(0, n_steps, body, state)
    for i, (_, acc) in enumerate(state):
        o_ref[:, i * dv:(i + 1) * dv] = (acc[:, :dv] / acc[:, dv:]).astype(o_ref.dtype)


def _attention(q, k, v, out, *, batch, n_ctx, n_lat, heads, kv_heads, dk, dv, q_col, k_col, v_col,
               q_tiles, cfg, name):
    t_rows = n_ctx + n_lat
    tq = n_ctx
    tpb = t_rows // tq
    q_off = 1 if q_tiles == "lat" else 0
    n_q = tpb - 1 if q_tiles == "lat" else 1
    group = heads // kv_heads
    tk = _divisor_tile(n_lat, cfg.kv_tile, tq)
    n_steps = n_lat // tk if q_tiles == "lat" else 0
    k_rows = t_rows if q_tiles == "lat" else n_ctx
    kpb = t_rows // k_rows
    hps = cfg.attn_heads_per_step
    assert heads % hps == 0 and q_col % hps == 0
    if group == 1:
        kvps = hps
        assert k_col % hps == 0 and v_col % hps == 0

        def kv_block(col, h):
            return col // hps + h
    else:
        kvps = 1
        assert group % hps == 0

        def kv_block(col, h):
            return col + (h * hps) // group

    def kern(q_ref, k_ref, v_ref, _, o_ref):
        _attn_kernel(q_ref, k_ref, v_ref, o_ref, n_ctx=n_ctx, n_steps=n_steps, tk=tk, hps=hps, kvps=kvps,
                     dk=dk, dv=dv)

    return pl.pallas_call(
        kern, out_shape=jax.ShapeDtypeStruct(out.shape, out.dtype),
        grid=(batch, heads // hps, n_q),
        in_specs=[pl.BlockSpec((tq, hps * dk), lambda b, h, i: (b * tpb + i + q_off, q_col // hps + h)),
                  pl.BlockSpec((k_rows, kvps * dk), lambda b, h, i: (b * kpb, kv_block(k_col, h))),
                  pl.BlockSpec((k_rows, kvps * dv), lambda b, h, i: (b * kpb, kv_block(v_col, h))),
                  pl.BlockSpec(memory_space=pl.ANY)],
        out_specs=pl.BlockSpec((tq, hps * dv), lambda b, h, i: (b * tpb + i + q_off, h)),
        input_output_aliases={3: 0},
        compiler_params=_params("arbitrary", "arbitrary", "arbitrary"), name=name,
    )(q, k, v, out)


def _natten_geometry(rows, kh, group):
    union = group + kh - 1
    assert rows % group == 0 and rows >= union
    base = np.clip(group * np.arange(rows // group) - kh // 2, 0, rows - union)
    return union, base


def _natten_kernel(type_ref, q_ref, k_ref, v_ref, bias_ref, o_ref, *, n_ctx, grid_w, rows, kh, group):
    del type_ref
    union, _ = _natten_geometry(rows, kh, group)
    base = jnp.clip(group * pl.program_id(2) - kh // 2, 0, rows - union)
    start = pl.multiple_of(n_ctx + base * grid_w, grid_w)
    q = q_ref[...]
    dn = (((1,), (1,)), ((), ()))
    k_loc = k_ref[pl.ds(start, union * grid_w), :]
    v_loc = v_ref[pl.ds(start, union * grid_w), :]
    s_loc = lax.dot_general(q, k_loc, dn, preferred_element_type=F32) + bias_ref[0, 0]
    s_ctx = lax.dot_general(q, k_ref[pl.ds(0, n_ctx), :], dn, preferred_element_type=F32)
    m = jnp.maximum(jnp.max(s_loc, axis=-1, keepdims=True), jnp.max(s_ctx, axis=-1, keepdims=True))
    p_loc = jnp.exp2(s_loc - m)
    p_ctx = jnp.exp2(s_ctx - m)
    l = jnp.sum(p_loc, axis=-1, keepdims=True) + jnp.sum(p_ctx, axis=-1, keepdims=True)
    acc = jnp.dot(p_loc.astype(BF16), v_loc, preferred_element_type=F32)
    acc = acc + jnp.dot(p_ctx.astype(BF16), v_ref[pl.ds(0, n_ctx), :], preferred_element_type=F32)
    o_ref[...] = (acc / l).astype(o_ref.dtype)


def _natten_bias(rpb, rows, grid_w, kh, kw, group, scale):
    full_kh = (rpb.shape[1] + 1) // 2
    union, base = _natten_geometry(rows, kh, group)
    n_groups = rows // group
    r = group * np.arange(n_groups)[:, None] + np.arange(group)[None, :]
    r0 = np.clip(r - kh // 2, 0, rows - kh)
    key_row = base[:, None] + np.arange(union)[None, :]
    valid = (key_row[:, None, :] >= r0[:, :, None]) & (key_row[:, None, :] < r0[:, :, None] + kh)
    assert (valid.sum(-1) == kh).all()
    dr = np.where(valid, key_row[:, None, :] - r[:, :, None] + full_kh - 1, -1)
    patterns, type_of_group = np.unique(dr.reshape(n_groups, -1), axis=0, return_inverse=True)
    dr = patterns.reshape(-1, group, union)
    row_hot = (dr[..., None] == np.arange(2 * full_kh - 1)).astype(np.float32)
    col = np.arange(grid_w)
    col0 = np.clip(col - kw // 2, 0, grid_w - kw)
    inside = (col[None, :] >= col0[:, None]) & (col[None, :] < col0[:, None] + kw)
    dc = col[None, :] - col[:, None] + kw - 1
    col_hot = ((dc[..., None] == np.arange(2 * kw - 1)) & inside[..., None]).astype(np.float32)
    hi = lax.Precision.HIGHEST
    t = jnp.einsum("trus,hsv->htruv", jnp.asarray(row_hot), rpb.astype(F32), precision=hi)
    b = jnp.einsum("htruv,ckv->htrcuk", t, jnp.asarray(col_hot), precision=hi)
    mask = (dr >= 0)[None, :, :, None, :, None] & inside[None, None, None, :, None, :]
    b = jnp.where(jnp.asarray(mask), b * scale, NEG)
    b = b.reshape(rpb.shape[0], dr.shape[0], group * grid_w, union * grid_w)
    return b, jnp.asarray(type_of_group.reshape(-1), jnp.int32)


def _natten(qkv, out, bias, group_type, *, batch, n_ctx, n_lat, heads, hd, grid_w, kh, name):
    t_rows = n_ctx + n_lat
    rows = n_lat // grid_w
    assert n_ctx % grid_w == 0
    group = n_ctx // grid_w
    tpb = t_rows // n_ctx
    union, _ = _natten_geometry(rows, kh, group)

    def kern(type_ref, q_ref, k_ref, v_ref, bias_ref, _, o_ref):
        _natten_kernel(type_ref, q_ref, k_ref, v_ref, bias_ref, o_ref, n_ctx=n_ctx, grid_w=grid_w, rows=rows,
                       kh=kh, group=group)

    return pl.pallas_call(
        kern, out_shape=jax.ShapeDtypeStruct(out.shape, out.dtype),
        grid_spec=pltpu.PrefetchScalarGridSpec(
            num_scalar_prefetch=1, grid=(batch, heads, rows // group),
            in_specs=[pl.BlockSpec((n_ctx, hd), lambda b, h, g, ty: (b * tpb + 1 + g, h)),
                      pl.BlockSpec((t_rows, hd), lambda b, h, g, ty: (b, heads + h)),
                      pl.BlockSpec((t_rows, hd), lambda b, h, g, ty: (b, 2 * heads + h)),
                      pl.BlockSpec((1, 1, n_ctx, union * grid_w), lambda b, h, g, ty: (h, ty[g], 0, 0)),
                      pl.BlockSpec(memory_space=pl.ANY)],
            out_specs=pl.BlockSpec((n_ctx, hd), lambda b, h, g, ty: (b * tpb + 1 + g, h))),
        input_output_aliases={5: 0},
        compiler_params=_params("arbitrary", "arbitrary", "arbitrary"), name=name,
    )(group_type, qkv, qkv, qkv, bias, out)


def _merge_kernel(h_ref, oa_ref, ob_ref, oc_ref, wg_ref, wb_ref, o_ref):
    h = h_ref[...]
    acc = None
    for i, o in enumerate((oa_ref, ob_ref, oc_ref)):
        gate = jax.nn.sigmoid(jnp.dot(h, wg_ref[i], preferred_element_type=F32))
        term = gate * jnp.dot(o[...], wb_ref[i], preferred_element_type=F32)
        acc = term if acc is None else acc + term
    o_ref[...] = acc.astype(o_ref.dtype)


def _merge(h, o_a, o_b, o_c, wg, wb, *, tm, tn):
    n_rows, d = h.shape
    bw = o_a.shape[1]
    nb = wg.shape[0]
    return pl.pallas_call(
        _merge_kernel, out_shape=jax.ShapeDtypeStruct((n_rows, d), BF16),
        grid=(n_rows // tm, d // tn),
        in_specs=[pl.BlockSpec((tm, d), lambda i, j: (i, 0))]
        + [pl.BlockSpec((tm, bw), lambda i, j: (i, 0))] * 3
        + [pl.BlockSpec((nb, d, tn), lambda i, j: (0, 0, j)),
           pl.BlockSpec((nb, bw, tn), lambda i, j: (0, 0, j))],
        out_specs=pl.BlockSpec((tm, tn), lambda i, j: (i, j)),
        compiler_params=_params("arbitrary", "arbitrary"), name="gated_merge",
    )(h, o_a, o_b, o_c, wg, wb)


def _deinterleave_kernel(w_ref, sel_ref, o_ref):
    sel = sel_ref[...]
    for c in range(w_ref.shape[3] // (2 * LANE)):
        sl = slice(c * 2 * LANE, (c + 1) * 2 * LANE)
        blk = w_ref[0, 0, :, sl].astype(BF16)
        o_ref[0, :, sl] = jnp.dot(blk, sel, preferred_element_type=F32).astype(BF16)


def _deinterleave_up_weights(w1_all, layer):
    _, n_exp, d, f2 = w1_all.shape
    assert f2 % (2 * LANE) == 0
    tr = _divisor_tile(d, 256, 8)
    sel = np.zeros((2 * LANE, 2 * LANE), np.float32)
    sel[2 * np.arange(LANE), np.arange(LANE)] = 1.0
    sel[2 * np.arange(LANE) + 1, LANE + np.arange(LANE)] = 1.0
    return pl.pallas_call(
        _deinterleave_kernel, out_shape=jax.ShapeDtypeStruct((n_exp, d, f2), BF16),
        grid=(n_exp, d // tr),
        in_specs=[pl.BlockSpec((1, 1, tr, f2), lambda e, r: (layer, e, r, 0)),
                  pl.BlockSpec((2 * LANE, 2 * LANE), lambda e, r: (0, 0))],
        out_specs=pl.BlockSpec((1, tr, f2), lambda e, r: (e, r, 0)),
        compiler_params=_params("arbitrary", "arbitrary"), name="moe_weight_deinterleave",
    )(w1_all, jnp.asarray(sel, BF16))


DMA_UNROLL = 8


def _dispatch_kernel(dest_ref, src_ref, _, dst_ref, sem, *, tm, top_k):
    base = pl.program_id(0) * tm

    def issue(c, _):
        for u in range(DMA_UNROLL):
            t = c * DMA_UNROLL + u
            for k in range(top_k):
                pltpu.make_async_copy(src_ref.at[t], dst_ref.at[dest_ref[(base + t) * top_k + k]], sem).start()
        return 0

    lax.fori_loop(0, tm // DMA_UNROLL, issue, 0)
    for _ in range(top_k):
        pltpu.make_async_copy(src_ref, dst_ref.at[pl.ds(0, tm)], sem).wait()


def _dispatch(dest, h3, xg0, *, tm, top_k):
    n_tok, s, lanes = h3.shape
    assert tm % DMA_UNROLL == 0
    return pl.pallas_call(
        functools.partial(_dispatch_kernel, tm=tm, top_k=top_k),
        out_shape=jax.ShapeDtypeStruct(xg0.shape, xg0.dtype),
        grid_spec=pltpu.PrefetchScalarGridSpec(
            num_scalar_prefetch=1, grid=(n_tok // tm,),
            in_specs=[pl.BlockSpec((tm, s, lanes), lambda i, dest: (i, 0, 0)),
                      pl.BlockSpec(memory_space=pl.ANY)],
            out_specs=pl.BlockSpec(memory_space=pl.ANY),
            scratch_shapes=[pltpu.SemaphoreType.DMA]),
        input_output_aliases={2: 0},
        compiler_params=pltpu.CompilerParams(dimension_semantics=("arbitrary",),
                                             vmem_limit_bytes=VMEM_LIMIT, disable_bounds_checks=True),
        name="moe_dispatch",
    )(dest, h3, xg0)


def _swiglu(x_glu, x_lin, alpha, limit):
    x_glu = jnp.minimum(x_glu, limit)
    x_lin = jnp.clip(x_lin, -limit, limit)
    return x_glu * jax.nn.sigmoid(alpha * x_glu) * (x_lin + 1.0)


def _moe_up_kernel(be_ref, nu_ref, x_ref, w_ref, b_ref, a_ref, *, tc, alpha, limit):
    i = pl.program_id(0)

    @pl.when(i < nu_ref[0])
    def _():
        x = x_ref[...]
        for c in range(w_ref.shape[2] // tc):
            sl = slice(c * tc, (c + 1) * tc)
            hdn = jnp.dot(x, w_ref[0, :, sl], preferred_element_type=F32) + b_ref[0, :, sl]
            for j in range(tc // (2 * LANE)):
                g = hdn[:, 2 * j * LANE:(2 * j + 1) * LANE]
                lin = hdn[:, (2 * j + 1) * LANE:(2 * j + 2) * LANE]
                col = (c * tc // (2 * LANE) + j) * LANE
                a_ref[:, col:col + LANE] = _swiglu(g, lin, alpha, limit).astype(a_ref.dtype)

    @pl.when(i >= nu_ref[0])
    def _():
        a_ref[...] = jnp.zeros(a_ref.shape, a_ref.dtype)


def _moe_down_kernel(be_ref, nu_ref, a_ref, w_ref, b_ref, y_ref, w_bf16):
    i = pl.program_id(0)

    @pl.when((i == 0) | (be_ref[i] != be_ref[jnp.maximum(i - 1, 0)]))
    def _():
        w_bf16[...] = w_ref[0, 0].astype(BF16)

    @pl.when(i < nu_ref[0])
    def _():
        y_ref[...] = jnp.dot(a_ref[...], w_bf16[...], preferred_element_type=F32) + b_ref[0]

    @pl.when(i >= nu_ref[0])
    def _():
        y_ref[...] = jnp.zeros(y_ref.shape, y_ref.dtype)


def _moe_experts(blk_e, n_used, xg, w1, b1, w2_all, layer, b2, *, bm, cfg):
    d = xg.shape[1]
    n_exp, _, f2 = w1.shape
    ff = f2 // 2
    nb = blk_e.shape[0]
    p_rows = nb * bm
    tc = _divisor_tile(f2, 1024, 2 * LANE)
    a = pl.pallas_call(
        functools.partial(_moe_up_kernel, tc=tc, alpha=cfg.swiglu_alpha, limit=cfg.swiglu_limit),
        out_shape=jax.ShapeDtypeStruct((p_rows, ff), BF16),
        grid_spec=pltpu.PrefetchScalarGridSpec(
            num_scalar_prefetch=2, grid=(nb,),
            in_specs=[pl.BlockSpec((bm, d), lambda i, be, nu: (i, 0)),
                      pl.BlockSpec((1, d, f2), lambda i, be, nu: (be[i], 0, 0)),
                      pl.BlockSpec((1, 1, f2), lambda i, be, nu: (be[i], 0, 0))],
            out_specs=pl.BlockSpec((bm, ff), lambda i, be, nu: (i, 0))),
        compiler_params=_params("arbitrary"), name="moe_up_swiglu",
    )(blk_e, n_used, xg, w1, b1)
    return pl.pallas_call(
        _moe_down_kernel,
        out_shape=jax.ShapeDtypeStruct((p_rows, d), F32),
        grid_spec=pltpu.PrefetchScalarGridSpec(
            num_scalar_prefetch=2, grid=(nb,),
            in_specs=[pl.BlockSpec((bm, ff), lambda i, be, nu: (i, 0)),
                      pl.BlockSpec((1, 1, ff, d), lambda i, be, nu: (layer, be[i], 0, 0)),
                      pl.BlockSpec((1, 1, d), lambda i, be, nu: (be[i], 0, 0))],
            out_specs=pl.BlockSpec((bm, d), lambda i, be, nu: (i, 0)),
            scratch_shapes=[pltpu.VMEM((ff, d), BF16)]),
        compiler_params=_params("arbitrary"), name="moe_down",
    )(blk_e, n_used, a, w2_all, b2)


def _combine_kernel(dest_ref, y_ref, w_ref, o_ref, buf, sem, *, tm, top_k):
    base = pl.program_id(0) * tm

    def issue(c, _):
        for u in range(DMA_UNROLL):
            t = c * DMA_UNROLL + u
            for k in range(top_k):
                pltpu.make_async_copy(y_ref.at[dest_ref[(base + t) * top_k + k]], buf.at[k, t], sem).start()
        return 0

    lax.fori_loop(0, tm // DMA_UNROLL, issue, 0)
    for k in range(top_k):
        pltpu.make_async_copy(y_ref.at[pl.ds(0, tm)], buf.at[k], sem).wait()
    acc = buf[0] * w_ref[0]
    for k in range(1, top_k):
        acc = acc + buf[k] * w_ref[k]
    o_ref[...] = acc


def _combine(dest, y3, w_rep, *, tm, top_k):
    _, s, lanes = y3.shape
    n_tok = w_rep.shape[1]
    return pl.pallas_call(
        functools.partial(_combine_kernel, tm=tm, top_k=top_k),
        out_shape=jax.ShapeDtypeStruct((n_tok, s, lanes), F32),
        grid_spec=pltpu.PrefetchScalarGridSpec(
            num_scalar_prefetch=1, grid=(n_tok // tm,),
            in_specs=[pl.BlockSpec(memory_space=pl.ANY),
                      pl.BlockSpec((top_k, tm, 1, lanes), lambda i, dest: (0, i, 0, 0))],
            out_specs=pl.BlockSpec((tm, s, lanes), lambda i, dest: (i, 0, 0)),
            scratch_shapes=[pltpu.VMEM((top_k, tm, s, lanes), F32), pltpu.SemaphoreType.DMA]),
        compiler_params=pltpu.CompilerParams(dimension_semantics=("arbitrary",),
                                             vmem_limit_bytes=VMEM_LIMIT, disable_bounds_checks=True),
        name="moe_combine",
    )(dest, y3, w_rep)


def _moe(h, top_i, top_w, valid, w1, b1, w2_all, layer, b2, *, cfg):
    n_tok, d = h.shape
    n_exp = w1.shape[0]
    kk = cfg.top_k
    bm = cfg.moe_rows
    idx = top_i[:, :kk]
    sel = (idx[:, :, None] == jnp.arange(n_exp, dtype=jnp.int32)[None, None, :]) & valid[:, None, None]
    sel = jnp.any(sel, axis=1).astype(jnp.int32)
    counts = jnp.sum(sel, axis=0)
    rank = jnp.cumsum(sel, axis=0) - sel
    padded = (counts + bm - 1) // bm * bm
    gend = jnp.cumsum(padded)
    gstart = gend - padded
    onehot = (idx[:, :, None] == jnp.arange(n_exp, dtype=jnp.int32)[None, None, :]).astype(jnp.int32)
    dest = jnp.sum(onehot * (gstart[None, None, :] + rank[:, None, :]), axis=-1)
    n_blocks = -(-(n_tok * kk) // bm) + n_exp
    p_rows = n_blocks * bm
    starts = jnp.arange(n_blocks, dtype=jnp.int32) * bm
    blk_e = jnp.minimum(jnp.sum(gend[None, :] <= starts[:, None], axis=1), n_exp - 1).astype(jnp.int32)
    n_used = (gend[-1] // bm).astype(jnp.int32).reshape(1)

    s = d // LANE
    tmd = _divisor_tile(n_tok, cfg.gather_rows, 16)
    spare = p_rows + jnp.arange(n_tok * kk, dtype=jnp.int32).reshape(n_tok, kk) % (tmd * kk)
    dest_w = jnp.where(valid[:, None], dest, spare).astype(jnp.int32).reshape(-1)
    xg = _dispatch(dest_w, h.reshape(n_tok, s, LANE), jnp.zeros((p_rows + tmd * kk, s, LANE), h.dtype),
                   tm=tmd, top_k=kk)
    y = _moe_experts(blk_e, n_used, xg.reshape(p_rows + tmd * kk, d), w1, b1, w2_all, layer, b2, bm=bm, cfg=cfg)
    dest_r = jnp.where(valid[:, None], dest, 0).astype(jnp.int32).reshape(-1)
    w_rep = jnp.broadcast_to(top_w[:, :kk].T[:, :, None, None], (kk, n_tok, 1, LANE))
    tmc = _divisor_tile(n_tok, cfg.combine_rows, 8)
    f = _combine(dest_r, y.reshape(p_rows, s, LANE), w_rep, tm=tmc, top_k=kk)
    return f.reshape(n_tok, d)


def _final_kernel(x_ref, f_ref, gate_ref, g_ref, o_ref, *, eps):
    x = x_ref[...] + gate_ref[0] * f_ref[...]
    ms = jnp.mean(x * x, axis=-1, keepdims=True)
    o_ref[0] = x * lax.rsqrt(ms + eps) * g_ref[...]


def _final_norm(x, f, gate, g, *, batch, n_ctx, n_lat, eps):
    d = x.shape[1]
    ts = n_ctx
    tpb = (n_ctx + n_lat) // ts
    row = pl.BlockSpec((ts, d), lambda b, s: (b * tpb + 1 + s, 0))
    return pl.pallas_call(
        functools.partial(_final_kernel, eps=eps),
        out_shape=jax.ShapeDtypeStruct((batch, n_lat, d), F32),
        grid=(batch, n_lat // ts),
        in_specs=[row, row, pl.BlockSpec((1, 1, d), lambda b, s: (b, 0, 0)),
                  pl.BlockSpec((1, d), lambda b, s: (0, 0))],
        out_specs=pl.BlockSpec((1, ts, d), lambda b, s: (b, s, 0)),
        compiler_params=_params("arbitrary", "arbitrary"), name="final_norm",
    )(x, f, gate, g.reshape(1, d))


def _rope_tables(kind, n_ctx, n_lat, grid_w, theta, dims):
    half = dims // 2
    nf = half // 2
    freqs = theta ** (-np.arange(0, half, 2, dtype=np.float64) / half)
    t = np.arange(n_lat)
    pos_r = np.concatenate([np.zeros(n_ctx), t // grid_w]).astype(np.float64)
    pos_c = np.concatenate([np.zeros(n_ctx), t % grid_w]).astype(np.float64)
    ang_r = pos_r[:, None] * freqs[None, :]
    ang_c = pos_c[:, None] * freqs[None, :]
    ang = np.concatenate([ang_r, ang_c], axis=1)
    rows = n_ctx + n_lat
    cos = np.ones((rows, LANE), np.float32)
    sin = np.zeros((rows, LANE), np.float32)
    w = 2 * nf
    cos[:, :w] = np.cos(ang)
    cos[:, LANE // 2:LANE // 2 + w] = np.cos(ang)
    sin[:, :w] = -np.sin(ang)
    sin[:, LANE // 2:LANE // 2 + w] = np.sin(ang)
    del kind
    return jnp.asarray(cos), jnp.asarray(sin)


def _axial_lanes(w, dims):
    nf = dims // 4
    lead = w.shape[:-1]
    w = jnp.swapaxes(w.reshape(lead + (2, 2, nf)), -3, -2).reshape(lead + (2, 2 * nf))
    w = jnp.pad(w, [(0, 0)] * (len(lead) + 1) + [(0, LANE // 2 - 2 * nf)])
    return w.reshape(lead + (LANE,))


def _forward(x, c, ctx, c_ctx, w_mod, b_mod, g_mix, w_in, g_q_a, w_uq, g_kv_a, w_ukv, g_qn, g_kn,
             rpb, w_branch, w_out, g_ffn, w_router, b_router, w_exp1, b_exp1, w_exp2, b_exp2, g_final,
             cfg):
    batch, n_lat, d = x.shape
    n_ctx = ctx.shape[1]
    depth = w_mod.shape[0]
    t_rows = n_ctx + n_lat
    n_rows = batch * t_rows
    hd = cfg.head_dim
    q_rank, kv_rank = g_q_a.shape[1], g_kv_a.shape[1]
    ha, hb, hkv, hc = cfg.mla_heads, cfg.gqa_heads, cfg.gqa_kv_heads, cfg.na_heads
    nope, rdim, vdim = cfg.mla_nope, cfg.mla_rope, cfg.mla_v
    assert nope == LANE and vdim == LANE and hd == LANE and rdim <= LANE // 2
    assert n_lat % n_ctx == 0 and n_ctx % 16 == 0 and batch <= 7
    kh_full = (rpb.shape[2] + 1) // 2
    kw = (rpb.shape[3] + 1) // 2
    rows = n_lat // cfg.grid_w
    kh = min(kh_full, rows)
    n_branch = w_branch.shape[1]
    assert n_branch == 3

    tm = _divisor_tile(t_rows, cfg.row_tile, 16)
    tpb = t_rows // tm

    sizes = (q_rank, kv_rank, rdim, hb * hd, hkv * hd, hkv * hd, hc * hd, hc * hd, hc * hd, n_branch * d)
    offs = np.concatenate([[0], np.cumsum(sizes)])

    cos_a, sin_a = _rope_tables("mla", n_ctx, n_lat, cfg.grid_w, cfg.rope_theta, rdim)
    cos_b, sin_b = _rope_tables("gqa", n_ctx, n_lat, cfg.grid_w, cfg.rope_theta, hd)
    rope_spec = pl.BlockSpec((tm, LANE), lambda i, j: (i % tpb, 0))
    log2e = float(np.log2(np.e))
    mla_scale = float((nope + rdim) ** -0.5) * log2e
    head_scale = float(hd ** -0.5) * log2e

    cc = jnp.zeros((8, d), F32).at[:batch].set(c).at[batch].set(c_ctx)
    mod = _modulation(cc, w_mod, b_mod)

    def table(layer, chunks):
        rows_ = []
        for ch in chunks:
            sl = mod[layer, :, ch * d:(ch + 1) * d]
            rows_.append(jnp.broadcast_to(sl[batch][None], (batch, d)))
            rows_.append(sl[:batch])
        tab = jnp.stack(rows_, axis=1)
        return jnp.pad(tab, ((0, 0), (0, 8 - tab.shape[1]), (0, 0)))

    xa = jnp.concatenate([ctx, x], axis=1).reshape(n_rows, d)
    pend_f = None
    pend_tab = None
    is_lat = (jnp.arange(n_rows, dtype=jnp.int32) % t_rows) >= n_ctx

    for layer in range(depth):
        last = layer == depth - 1
        wi = w_in[layer]

        w_d = wi[:, offs[0]:offs[2]].astype(BF16)
        g_d = jnp.concatenate([g_q_a[layer], g_kv_a[layer]]).reshape(1, -1)
        w_kr = _axial_lanes(wi[:, offs[2]:offs[3]], rdim).astype(BF16)
        uq = w_uq[layer].reshape(q_rank, ha, nope + rdim)
        uq = jnp.concatenate([uq[:, :, :nope], _axial_lanes(uq[:, :, nope:], rdim)], axis=-1)
        uq = uq.reshape(q_rank, ha * 2 * LANE).astype(BF16)
        ukv = w_ukv[layer].reshape(kv_rank, ha, nope + vdim)
        uk = ukv[:, :, :nope].reshape(kv_rank, ha * nope).astype(BF16)
        uv = ukv[:, :, nope:].reshape(kv_rank, ha * vdim).astype(BF16)
        w_qb = _axial_lanes(wi[:, offs[3]:offs[4]].reshape(d, hb, hd), hd).reshape(d, hb * hd).astype(BF16)
        w_kb = _axial_lanes(wi[:, offs[4]:offs[5]].reshape(d, hkv, hd), hd).reshape(d, hkv * hd).astype(BF16)
        w_vb = wi[:, offs[5]:offs[6]].astype(BF16)
        g_qn_p = _axial_lanes(g_qn[layer], hd).reshape(1, hd)
        g_kn_p = _axial_lanes(g_kn[layer], hd).reshape(1, hd)
        w_c = wi[:, offs[6]:offs[9]].astype(BF16)
        c_scale = jnp.concatenate([jnp.full((hc * hd,), head_scale, F32),
                                   jnp.ones((2 * hc * hd,), F32)]).reshape(1, -1)
        w_g = wi[:, offs[9]:offs[10]].reshape(d, n_branch, d).transpose(1, 0, 2).astype(BF16)
        w_b = w_branch[layer].astype(BF16)
        w_o = w_out[layer].astype(BF16)
        ff = w_exp2.shape[2]
        w1 = _deinterleave_up_weights(w_exp1, layer)
        b1 = b_exp1[layer].reshape(-1, ff // LANE, LANE, 2).transpose(0, 1, 3, 2).reshape(-1, 1, 2 * ff)
        b2 = b_exp2[layer].reshape(-1, 1, d)

        tab1 = table(layer, (0, 1)) if pend_f is None else jnp.concatenate(
            [table(layer, (0, 1))[:, :4], pend_tab[:, :2], jnp.zeros((batch, 2, d), F32)], axis=1)
        res = _norm_mod(xa, pend_f, tab1, g_mix[layer], None, t_rows=t_rows, n_ctx=n_ctx, cfg=cfg)
        if pend_f is not None:
            xa, h = res
        else:
            (h,) = res

        dn = _matmul(h, w_d, functools.partial(_epi_rmsnorm, eps=cfg.eps),
                     [(g_d, pl.BlockSpec((1, q_rank), lambda i, j: (0, j)))],
                     jax.ShapeDtypeStruct((n_rows, q_rank + kv_rank), BF16),
                     pl.BlockSpec((tm, q_rank), lambda i, j: (i, j)), tm=tm, tn=q_rank, name="mla_down")
        assert q_rank == kv_rank
        k_pe = _matmul(h, w_kr,
                       functools.partial(_epi_rope_tiles, rope_tiles=(True,), norm=False, eps=cfg.eps, scale=1.0),
                       [(cos_a, rope_spec), (sin_a, rope_spec)],
                       jax.ShapeDtypeStruct((n_rows, LANE), BF16),
                       pl.BlockSpec((tm, LANE), lambda i, j: (i, 0)), tm=tm, tn=LANE, name="mla_rope_key")
        hpt = 2 if ha % 2 == 0 else 1
        q_a = _matmul(dn, uq,
                      functools.partial(_epi_rope_tiles, rope_tiles=(False, True), norm=False, eps=cfg.eps,
                                        scale=mla_scale),
                      [(cos_a, rope_spec), (sin_a, rope_spec)],
                      jax.ShapeDtypeStruct((n_rows, ha * 2 * LANE), BF16),
                      pl.BlockSpec((tm, hpt * 2 * LANE), lambda i, j: (i, j)), tm=tm, tn=hpt * 2 * LANE, a_col=0,
                      name="mla_q_up")
        k_a = _matmul(dn, uk, _epi_mla_key,
                      [(k_pe, pl.BlockSpec((tm, LANE), lambda i, j: (i, 0)))],
                      jax.ShapeDtypeStruct((n_rows, ha * 2 * LANE), BF16),
                      pl.BlockSpec((tm, hpt * 2 * LANE), lambda i, j: (i, j)), tm=tm, tn=hpt * LANE, a_col=1,
                      name="mla_k_up")
        tn_v = _divisor_tile(ha * vdim, 4 * LANE, LANE)
        v_a = _matmul(dn, uv, _epi_plain, [],
                      jax.ShapeDtypeStruct((n_rows, ha * vdim), BF16),
                      pl.BlockSpec((tm, tn_v), lambda i, j: (i, j)), tm=tm, tn=tn_v, a_col=1, name="mla_v_up")

        tn_q = _divisor_tile(hb * hd, 4 * LANE, LANE)
        q_b = _matmul(h, w_qb,
                      functools.partial(_epi_rope_tiles, rope_tiles=(True,), norm=True, eps=cfg.eps,
                                        scale=head_scale),
                      [(g_qn_p, pl.BlockSpec((1, hd), lambda i, j: (0, 0))), (cos_b, rope_spec), (sin_b, rope_spec)],
                      jax.ShapeDtypeStruct((n_rows, hb * hd), BF16),
                      pl.BlockSpec((tm, tn_q), lambda i, j: (i, j)), tm=tm, tn=tn_q, name="gqa_q")
        tn_k = _divisor_tile(hkv * hd, 4 * LANE, LANE)
        k_b = _matmul(h, w_kb,
                      functools.partial(_epi_rope_tiles, rope_tiles=(True,), norm=True, eps=cfg.eps, scale=1.0),
                      [(g_kn_p, pl.BlockSpec((1, hd), lambda i, j: (0, 0))), (cos_b, rope_spec), (sin_b, rope_spec)],
                      jax.ShapeDtypeStruct((n_rows, hkv * hd), BF16),
                      pl.BlockSpec((tm, tn_k), lambda i, j: (i, j)), tm=tm, tn=tn_k, name="gqa_k")
        v_b = _matmul(h, w_vb, _epi_plain, [],
                      jax.ShapeDtypeStruct((n_rows, hkv * hd), BF16),
                      pl.BlockSpec((tm, tn_k), lambda i, j: (i, j)), tm=tm, tn=tn_k, name="gqa_v")

        tn_c = _divisor_tile(3 * hc * hd, 4 * LANE, LANE)
        qkv_c = _matmul(h, w_c, _epi_scale, [(c_scale, pl.BlockSpec((1, tn_c), lambda i, j: (0, j)))],
                        jax.ShapeDtypeStruct((n_rows, 3 * hc * hd), BF16),
                        pl.BlockSpec((tm, tn_c), lambda i, j: (i, j)), tm=tm, tn=tn_c, name="natten_qkv")

        common = dict(batch=batch, n_ctx=n_ctx, n_lat=n_lat, cfg=cfg)
        assert ha * vdim == hb * hd == hc * hd
        o_a = o_b = o_c = jnp.zeros((n_rows, hc * hd), BF16)
        spec_a = dict(heads=ha, kv_heads=ha, dk=2 * LANE, dv=vdim, q_col=0, k_col=0, v_col=0)
        spec_b = dict(heads=hb, kv_heads=hkv, dk=hd, dv=hd, q_col=0, k_col=0, v_col=0)
        if not last:
            o_a = _attention(q_a, k_a, v_a, o_a, q_tiles="ctx", name="mla_ctx_attention", **spec_a, **common)
            o_b = _attention(q_b, k_b, v_b, o_b, q_tiles="ctx", name="gqa_ctx_attention", **spec_b, **common)
            o_c = _attention(qkv_c, qkv_c, qkv_c, o_c, heads=hc, kv_heads=hc, dk=hd, dv=hd,
                             q_col=0, k_col=hc, v_col=2 * hc, q_tiles="ctx", name="natten_ctx_attention",
                             **common)
        o_a = _attention(q_a, k_a, v_a, o_a, q_tiles="lat", name="mla_attention", **spec_a, **common)
        o_b = _attention(q_b, k_b, v_b, o_b, q_tiles="lat", name="gqa_attention", **spec_b, **common)
        bias, group_type = _natten_bias(rpb[layer], rows, cfg.grid_w, kh, kw, n_ctx // cfg.grid_w, log2e)
        o_c = _natten(qkv_c, o_c, bias, group_type, batch=batch, n_ctx=n_ctx, n_lat=n_lat, heads=hc, hd=hd,
                      grid_w=cfg.grid_w, kh=kh, name="natten")

        tn_m = _divisor_tile(d, 4 * LANE, LANE)
        merged = _merge(h, o_a, o_b, o_c, w_g, w_b, tm=tm, tn=tn_m)
        gate1 = table(layer, (2,))
        x1 = _matmul(merged, w_o, functools.partial(_epi_residual, tiles_per_batch=tpb, tm=tm, n_ctx=n_ctx),
                     [(xa, pl.BlockSpec((tm, tn_m), lambda i, j: (i, j))),
                      (gate1, pl.BlockSpec((1, 8, tn_m), lambda i, j: (i // tpb, 0, j)))],
                     jax.ShapeDtypeStruct((n_rows, d), F32),
                     pl.BlockSpec((tm, tn_m), lambda i, j: (i, j)), tm=tm, tn=tn_m, name="out_proj_residual")

        h2, top_i, top_w = _norm_mod(x1, None, table(layer, (3, 4)), g_ffn[layer],
                                     (w_router[layer], b_router[layer]), t_rows=t_rows, n_ctx=n_ctx, cfg=cfg)
        valid = is_lat if last else jnp.ones((n_rows,), bool)
        pend_f = _moe(h2, top_i, top_w, valid, w1, b1, w_exp2, layer, b2, cfg=cfg)
        pend_tab = table(layer, (5,))
        xa = x1

    gate2 = pend_tab[:, 1:2]
    return _final_norm(xa, pend_f, gate2, g_final, batch=batch, n_ctx=n_ctx, n_lat=n_lat, eps=cfg.eps)


def kernel(x, c, ctx, c_ctx, w_mod, b_mod, g_mix, w_in, g_q_a, w_uq, g_kv_a, w_ukv, g_qn, g_kn, rpb,
           w_branch, w_out, g_ffn, w_router, b_router, w_exp1, b_exp1, w_exp2, b_exp2, g_final):
    return _forward(x, c, ctx, c_ctx, w_mod, b_mod, g_mix, w_in, g_q_a, w_uq, g_kv_a, w_ukv, g_qn, g_kn,
                    rpb, w_branch, w_out, g_ffn, w_router, b_router, w_exp1, b_exp1, w_exp2, b_exp2,
                    g_final, CFG)
```

```python
import functools
from typing import NamedTuple

import numpy as np
import jax
import jax.numpy as jnp
from jax import lax
from jax.experimental import pallas as pl
from jax.experimental.pallas import tpu as pltpu

F32 = jnp.float32
BF16 = jnp.bfloat16

LANE = 128
VMEM_LIMIT = 56 * 1024 * 1024
NEG = -1e30


class Config(NamedTuple):
    grid_w: int = 64
    eps: float = 1e-6
    rope_theta: float = 10000.0
    mla_heads: int = 8
    mla_nope: int = 128
    mla_rope: int = 64
    mla_v: int = 128
    gqa_heads: int = 8
    gqa_kv_heads: int = 2
    head_dim: int = 128
    na_heads: int = 8
    top_k: int = 4
    swiglu_alpha: float = 1.702
    swiglu_limit: float = 7.0
    row_tile: int = 1100
    kv_tile: int = 1024
    attn_heads_per_step: int = 4
    moe_rows: int = 256
    gather_rows: int = 512
    combine_rows: int = 128


CFG = Config()


def _divisor_tile(n, target, mult):
    best = None
    for d in range(mult, min(n, target) + 1, mult):
        if n % d == 0:
            best = d
    assert best is not None, (n, target, mult)
    return best


def _params(*sem):
    return pltpu.CompilerParams(dimension_semantics=sem, vmem_limit_bytes=VMEM_LIMIT)


def _mod_kernel(c_ref, w_ref, b_ref, o_ref):
    c = c_ref[...]
    a = (c * jax.nn.sigmoid(c)).astype(BF16)
    o_ref[0] = jnp.dot(a, w_ref[0].astype(BF16), preferred_element_type=F32) + b_ref[0]


def _modulation(cc, w_mod, b_mod):
    depth, d, n6 = w_mod.shape
    tn = _divisor_tile(n6, 1024, LANE)
    return pl.pallas_call(
        _mod_kernel,
        out_shape=jax.ShapeDtypeStruct((depth, 8, n6), F32),
        grid=(depth, n6 // tn),
        in_specs=[pl.BlockSpec((8, d), lambda l, j: (0, 0)),
                  pl.BlockSpec((1, d, tn), lambda l, j: (l, 0, j)),
                  pl.BlockSpec((1, 1, tn), lambda l, j: (l, 0, j))],
        out_specs=pl.BlockSpec((1, 8, tn), lambda l, j: (l, 0, j)),
        compiler_params=_params("arbitrary", "arbitrary"),
        name="adaln_modulation",
    )(cc, w_mod, b_mod.reshape(depth, 1, n6))


def _ctx_mask(tile_idx, tiles_per_batch, tm, n_ctx):
    row = (tile_idx % tiles_per_batch) * tm + lax.broadcasted_iota(jnp.int32, (tm, 1), 0)
    return row < n_ctx


def _norm_kernel(*refs, has_res, has_router, tiles_per_batch, tm, n_ctx, eps, top_k):
    it = iter(refs)
    x_ref = next(it)
    f_ref = next(it) if has_res else None
    tab_ref = next(it)
    g_ref = next(it)
    if has_router:
        wr_ref = next(it)
        br_ref = next(it)
    xo_ref = next(it) if has_res else None
    h_ref = next(it)
    if has_router:
        ti_ref = next(it)
        tw_ref = next(it)

    is_ctx = _ctx_mask(pl.program_id(0), tiles_per_batch, tm, n_ctx)
    tab = tab_ref[0]

    def pick(k):
        return jnp.where(is_ctx, tab[2 * k:2 * k + 1], tab[2 * k + 1:2 * k + 2])

    x = x_ref[...]
    if has_res:
        x = x + pick(2) * f_ref[...]
        xo_ref[...] = x
    ms = jnp.mean(x * x, axis=-1, keepdims=True)
    y = x * lax.rsqrt(ms + eps) * g_ref[...]
    n = y * (1.0 + pick(1)) + pick(0)
    h_ref[...] = n.astype(h_ref.dtype)

    if has_router:
        logits = jnp.dot(n, wr_ref[...], preferred_element_type=F32,
                         precision=lax.Precision.HIGHEST) + br_ref[...]
        lane = lax.broadcasted_iota(jnp.int32, logits.shape, 1)
        vals, idxs = [], []
        cur = logits
        for _ in range(top_k):
            m = jnp.max(cur, axis=-1, keepdims=True)
            idx = jnp.min(jnp.where(cur == m, lane, LANE), axis=-1, keepdims=True)
            vals.append(m)
            idxs.append(idx)
            cur = jnp.where(lane == idx, NEG, cur)
        es = [jnp.exp(v - vals[0]) for v in vals]
        den = es[0]
        for e in es[1:]:
            den = den + e
        ti = jnp.zeros(logits.shape, jnp.int32)
        tw = jnp.zeros(logits.shape, F32)
        for k in range(top_k):
            ti = jnp.where(lane == k, idxs[k], ti)
            tw = jnp.where(lane == k, es[k] / den, tw)
        ti_ref[...] = ti
        tw_ref[...] = tw


def _norm_mod(x, f, tab, g, router, *, t_rows, n_ctx, cfg):
    n_rows, d = x.shape
    tm = _divisor_tile(t_rows, 256, 8)
    tpb = t_rows // tm
    has_res = f is not None
    has_router = router is not None
    row = pl.BlockSpec((tm, d), lambda i: (i, 0))
    ins = [x] + ([f] if has_res else []) + [tab, g.reshape(1, d)]
    in_specs = [row] + ([row] if has_res else []) + [
        pl.BlockSpec((1, 8, d), lambda i: (i // tpb, 0, 0)),
        pl.BlockSpec((1, d), lambda i: (0, 0))]
    outs, out_specs = [], []
    if has_res:
        outs.append(jax.ShapeDtypeStruct((n_rows, d), F32))
        out_specs.append(row)
    outs.append(jax.ShapeDtypeStruct((n_rows, d), F32 if has_router else BF16))
    out_specs.append(row)
    if has_router:
        w_r, b_r = router
        n_exp = w_r.shape[1]
        assert n_exp <= LANE
        wr = jnp.zeros((d, LANE), F32).at[:, :n_exp].set(w_r)
        br = jnp.full((1, LANE), NEG, F32).at[0, :n_exp].set(b_r)
        ins += [wr, br]
        in_specs += [pl.BlockSpec((d, LANE), lambda i: (0, 0)), pl.BlockSpec((1, LANE), lambda i: (0, 0))]
        lane_row = pl.BlockSpec((tm, LANE), lambda i: (i, 0))
        outs += [jax.ShapeDtypeStruct((n_rows, LANE), jnp.int32), jax.ShapeDtypeStruct((n_rows, LANE), F32)]
        out_specs += [lane_row, lane_row]
    res = pl.pallas_call(
        functools.partial(_norm_kernel, has_res=has_res, has_router=has_router, tiles_per_batch=tpb,
                          tm=tm, n_ctx=n_ctx, eps=cfg.eps, top_k=cfg.top_k),
        out_shape=outs, grid=(n_rows // tm,), in_specs=in_specs, out_specs=out_specs,
        compiler_params=_params("arbitrary"),
        name="norm_modulate_router" if has_router else "norm_modulate",
    )(*ins)
    return list(res)


def _rope(t, cos, sin):
    return t * cos + pltpu.roll(t, LANE // 2, 1) * sin


def _mm_kernel(*refs, epilogue, n_extra):
    a_ref, w_ref = refs[0], refs[1]
    extra = refs[2:2 + n_extra]
    outs = refs[2 + n_extra:]
    acc = jnp.dot(a_ref[...], w_ref[...], preferred_element_type=F32)
    epilogue(acc, extra, outs)


def _matmul(a, w, epilogue, extras, out_shape, out_spec, *, tm, tn, a_col=0, name):
    m = a.shape[0]
    k, nw = w.shape
    in_specs = [pl.BlockSpec((tm, k), lambda i, j: (i, a_col)), pl.BlockSpec((k, tn), lambda i, j: (0, j))]
    in_specs += [s for _, s in extras]
    return pl.pallas_call(
        functools.partial(_mm_kernel, epilogue=epilogue, n_extra=len(extras)),
        out_shape=out_shape, grid=(m // tm, nw // tn), in_specs=in_specs, out_specs=out_spec,
        compiler_params=_params("arbitrary", "arbitrary"), name=name,
    )(a, w, *[x for x, _ in extras])


def _epi_rmsnorm(acc, extra, outs, *, eps):
    (g_ref,), (o_ref,) = extra, outs
    ms = jnp.mean(acc * acc, axis=-1, keepdims=True)
    o_ref[...] = (acc * lax.rsqrt(ms + eps) * g_ref[...]).astype(o_ref.dtype)


def _epi_scale(acc, extra, outs):
    (s_ref,), (o_ref,) = extra, outs
    o_ref[...] = (acc * s_ref[...]).astype(o_ref.dtype)


def _epi_plain(acc, extra, outs):
    outs[0][...] = acc.astype(outs[0].dtype)


def _epi_rope_tiles(acc, extra, outs, *, rope_tiles, norm, eps, scale):
    if norm:
        g_ref, cos_ref, sin_ref = extra
    else:
        cos_ref, sin_ref = extra
    o_ref = outs[0]
    cos, sin = cos_ref[...], sin_ref[...]
    for t in range(acc.shape[1] // LANE):
        v = acc[:, t * LANE:(t + 1) * LANE]
        if norm:
            ms = jnp.mean(v * v, axis=-1, keepdims=True)
            v = v * lax.rsqrt(ms + eps) * g_ref[...]
        if rope_tiles[t % len(rope_tiles)]:
            v = _rope(v, cos, sin)
        o_ref[:, t * LANE:(t + 1) * LANE] = (v * scale).astype(o_ref.dtype)


def _epi_mla_key(acc, extra, outs):
    (kpe_ref,), (o_ref,) = extra, outs
    for hh in range(acc.shape[1] // LANE):
        o_ref[:, 2 * hh * LANE:(2 * hh + 1) * LANE] = acc[:, hh * LANE:(hh + 1) * LANE].astype(o_ref.dtype)
        o_ref[:, (2 * hh + 1) * LANE:(2 * hh + 2) * LANE] = kpe_ref[...]


def _epi_residual(acc, extra, outs, *, tiles_per_batch, tm, n_ctx):
    (x_ref, tab_ref), (o_ref,) = extra, outs
    is_ctx = _ctx_mask(pl.program_id(0), tiles_per_batch, tm, n_ctx)
    tab = tab_ref[0]
    gate = jnp.where(is_ctx, tab[0:1], tab[1:2])
    o_ref[...] = x_ref[...] + gate * acc


def _attn_kernel(q_ref, k_ref, v_ref, o_ref, *, n_ctx, n_steps, tk, hps, kvps, dk, dv):
    tq = q_ref.shape[0]
    assert dv == LANE
    qs = [q_ref[:, i * dk:(i + 1) * dk] for i in range(hps)]

    def step(start, size, state):
        ones = jnp.ones((size, LANE), BF16)
        new = []
        for i, (m, acc) in enumerate(state):
            kv = i * kvps // hps
            k = k_ref[pl.ds(start, size), kv * dk:(kv + 1) * dk]
            v = jnp.concatenate([v_ref[pl.ds(start, size), kv * dv:(kv + 1) * dv], ones], axis=1)
            s = lax.dot_general(qs[i], k, (((1,), (1,)), ((), ())), preferred_element_type=F32)
            m_new = jnp.maximum(m, jnp.max(s, axis=-1, keepdims=True))
            p = jnp.exp2(s - m_new).astype(BF16)
            acc = jnp.exp2(m - m_new) * acc + jnp.dot(p, v, preferred_element_type=F32)
            new.append((m_new, acc))
        return tuple(new)

    state = tuple((jnp.full((tq, 1), NEG, F32), jnp.zeros((tq, dv + LANE), F32)) for _ in range(hps))
    if n_steps == 0:
        state = step(0, n_ctx, state)
    else:
        state = step(0, n_ctx + tk, state)
        if n_steps <= 4:
            for c in range(1, n_steps):
                state = step(n_ctx + c * tk, tk, state)
        else:
            def body(c, state):
                return step(pl.multiple_of(n_ctx + c * tk, n_ctx), tk, state)

            state = lax.fori_loop(1, n_steps, body, state)
    for i, (_, acc) in enumerate(state):
        o_ref[:, i * dv:(i + 1) * dv] = (acc[:, :dv] / acc[:, dv:]).astype(o_ref.dtype)


def _attention(q, k, v, out, *, batch, n_ctx, n_lat, heads, kv_heads, dk, dv, q_col, k_col, v_col,
               q_tiles, cfg, name):
    t_rows = n_ctx + n_lat
    tq = n_ctx
    tpb = t_rows // tq
    q_off = 1 if q_tiles == "lat" else 0
    n_q = tpb - 1 if q_tiles == "lat" else 1
    group = heads // kv_heads
    tk = _divisor_tile(n_lat, cfg.kv_tile, tq)
    n_steps = n_lat // tk if q_tiles == "lat" else 0
    k_rows = t_rows if q_tiles == "lat" else n_ctx
    kpb = t_rows // k_rows
    hps = cfg.attn_heads_per_step
    assert heads % hps == 0 and q_col % hps == 0
    if group == 1:
        kvps = hps
        assert k_col % hps == 0 and v_col % hps == 0

        def kv_block(col, h):
            return col // hps + h
    else:
        kvps = 1
        assert group % hps == 0

        def kv_block(col, h):
            return col + (h * hps) // group

    def kern(q_ref, k_ref, v_ref, _, o_ref):
        _attn_kernel(q_ref, k_ref, v_ref, o_ref, n_ctx=n_ctx, n_steps=n_steps, tk=tk, hps=hps, kvps=kvps,
                     dk=dk, dv=dv)

    return pl.pallas_call(
        kern, out_shape=jax.ShapeDtypeStruct(out.shape, out.dtype),
        grid=(batch, heads // hps, n_q),
        in_specs=[pl.BlockSpec((tq, hps * dk), lambda b, h, i: (b * tpb + i + q_off, q_col // hps + h)),
                  pl.BlockSpec((k_rows, kvps * dk), lambda b, h, i: (b * kpb, kv_block(k_col, h))),
                  pl.BlockSpec((k_rows, kvps * dv), lambda b, h, i: (b * kpb, kv_block(v_col, h))),
                  pl.BlockSpec(memory_space=pl.ANY)],
        out_specs=pl.BlockSpec((tq, hps * dv), lambda b, h, i: (b * tpb + i + q_off, h)),
        input_output_aliases={3: 0},
        compiler_params=_params("arbitrary", "arbitrary", "arbitrary"), name=name,
    )(q, k, v, out)


def _natten_geometry(rows, kh, group):
    union = group + kh - 1
    assert rows % group == 0 and rows >= union
    base = np.clip(group * np.arange(rows // group) - kh // 2, 0, rows - union)
    return union, base


def _natten_kernel(type_ref, q_ref, k_ref, v_ref, bias_ref, o_ref, *, n_ctx, grid_w, rows, kh, group):
    del type_ref
    union, _ = _natten_geometry(rows, kh, group)
    base = jnp.clip(group * pl.program_id(2) - kh // 2, 0, rows - union)
    start = pl.multiple_of(n_ctx + base * grid_w, grid_w)
    q = q_ref[...]
    dn = (((1,), (1,)), ((), ()))
    k_loc = k_ref[pl.ds(start, union * grid_w), :]
    v_loc = v_ref[pl.ds(start, union * grid_w), :]
    s_loc = lax.dot_general(q, k_loc, dn, preferred_element_type=F32) + bias_ref[0, 0]
    s_ctx = lax.dot_general(q, k_ref[pl.ds(0, n_ctx), :], dn, preferred_element_type=F32)
    m = jnp.maximum(jnp.max(s_loc, axis=-1, keepdims=True), jnp.max(s_ctx, axis=-1, keepdims=True))
    p_loc = jnp.exp2(s_loc - m)
    p_ctx = jnp.exp2(s_ctx - m)
    l = jnp.sum(p_loc, axis=-1, keepdims=True) + jnp.sum(p_ctx, axis=-1, keepdims=True)
    acc = jnp.dot(p_loc.astype(BF16), v_loc, preferred_element_type=F32)
    acc = acc + jnp.dot(p_ctx.astype(BF16), v_ref[pl.ds(0, n_ctx), :], preferred_element_type=F32)
    o_ref[...] = (acc / l).astype(o_ref.dtype)


def _natten_bias(rpb, rows, grid_w, kh, kw, group, scale):
    full_kh = (rpb.shape[1] + 1) // 2
    union, base = _natten_geometry(rows, kh, group)
    n_groups = rows // group
    r = group * np.arange(n_groups)[:, None] + np.arange(group)[None, :]
    r0 = np.clip(r - kh // 2, 0, rows - kh)
    key_row = base[:, None] + np.arange(union)[None, :]
    valid = (key_row[:, None, :] >= r0[:, :, None]) & (key_row[:, None, :] < r0[:, :, None] + kh)
    assert (valid.sum(-1) == kh).all()
    dr = np.where(valid, key_row[:, None, :] - r[:, :, None] + full_kh - 1, -1)
    patterns, type_of_group = np.unique(dr.reshape(n_groups, -1), axis=0, return_inverse=True)
    dr = patterns.reshape(-1, group, union)
    row_hot = (dr[..., None] == np.arange(2 * full_kh - 1)).astype(np.float32)
    col = np.arange(grid_w)
    col0 = np.clip(col - kw // 2, 0, grid_w - kw)
    inside = (col[None, :] >= col0[:, None]) & (col[None, :] < col0[:, None] + kw)
    dc = col[None, :] - col[:, None] + kw - 1
    col_hot = ((dc[..., None] == np.arange(2 * kw - 1)) & inside[..., None]).astype(np.float32)
    hi = lax.Precision.HIGHEST
    t = jnp.einsum("trus,hsv->htruv", jnp.asarray(row_hot), rpb.astype(F32), precision=hi)
    b = jnp.einsum("htruv,ckv->htrcuk", t, jnp.asarray(col_hot), precision=hi)
    mask = (dr >= 0)[None, :, :, None, :, None] & inside[None, None, None, :, None, :]
    b = jnp.where(jnp.asarray(mask), b * scale, NEG)
    b = b.reshape(rpb.shape[0], dr.shape[0], group * grid_w, union * grid_w)
    return b, jnp.asarray(type_of_group.reshape(-1), jnp.int32)


def _natten(qkv, out, bias, group_type, *, batch, n_ctx, n_lat, heads, hd, grid_w, kh, name):
    t_rows = n_ctx + n_lat
    rows = n_lat // grid_w
    assert n_ctx % grid_w == 0
    group = n_ctx // grid_w
    tpb = t_rows // n_ctx
    union, _ = _natten_geometry(rows, kh, group)

    def kern(type_ref, q_ref, k_ref, v_ref, bias_ref, _, o_ref):
        _natten_kernel(type_ref, q_ref, k_ref, v_ref, bias_ref, o_ref, n_ctx=n_ctx, grid_w=grid_w, rows=rows,
                       kh=kh, group=group)

    return pl.pallas_call(
        kern, out_shape=jax.ShapeDtypeStruct(out.shape, out.dtype),
        grid_spec=pltpu.PrefetchScalarGridSpec(
            num_scalar_prefetch=1, grid=(batch, heads, rows // group),
            in_specs=[pl.BlockSpec((n_ctx, hd), lambda b, h, g, ty: (b * tpb + 1 + g, h)),
                      pl.BlockSpec((t_rows, hd), lambda b, h, g, ty: (b, heads + h)),
                      pl.BlockSpec((t_rows, hd), lambda b, h, g, ty: (b, 2 * heads + h)),
                      pl.BlockSpec((1, 1, n_ctx, union * grid_w), lambda b, h, g, ty: (h, ty[g], 0, 0)),
                      pl.BlockSpec(memory_space=pl.ANY)],
            out_specs=pl.BlockSpec((n_ctx, hd), lambda b, h, g, ty: (b * tpb + 1 + g, h))),
        input_output_aliases={5: 0},
        compiler_params=_params("arbitrary", "arbitrary", "arbitrary"), name=name,
    )(group_type, qkv, qkv, qkv, bias, out)


def _merge_kernel(h_ref, oa_ref, ob_ref, oc_ref, wg_ref, wb_ref, o_ref):
    h = h_ref[...]
    acc = None
    for i, o in enumerate((oa_ref, ob_ref, oc_ref)):
        gate = jax.nn.sigmoid(jnp.dot(h, wg_ref[i], preferred_element_type=F32))
        term = gate * jnp.dot(o[...], wb_ref[i], preferred_element_type=F32)
        acc = term if acc is None else acc + term
    o_ref[...] = acc.astype(o_ref.dtype)


def _merge(h, o_a, o_b, o_c, wg, wb, *, tm, tn):
    n_rows, d = h.shape
    bw = o_a.shape[1]
    nb = wg.shape[0]
    return pl.pallas_call(
        _merge_kernel, out_shape=jax.ShapeDtypeStruct((n_rows, d), BF16),
        grid=(n_rows // tm, d // tn),
        in_specs=[pl.BlockSpec((tm, d), lambda i, j: (i, 0))]
        + [pl.BlockSpec((tm, bw), lambda i, j: (i, 0))] * 3
        + [pl.BlockSpec((nb, d, tn), lambda i, j: (0, 0, j)),
           pl.BlockSpec((nb, bw, tn), lambda i, j: (0, 0, j))],
        out_specs=pl.BlockSpec((tm, tn), lambda i, j: (i, j)),
        compiler_params=_params("arbitrary", "arbitrary"), name="gated_merge",
    )(h, o_a, o_b, o_c, wg, wb)


def _deinterleave_kernel(w_ref, sel_ref, o_ref):
    sel = sel_ref[...]
    for c in range(w_ref.shape[3] // (2 * LANE)):
        sl = slice(c * 2 * LANE, (c + 1) * 2 * LANE)
        blk = w_ref[0, 0, :, sl].astype(BF16)
        o_ref[0, :, sl] = jnp.dot(blk, sel, preferred_element_type=F32).astype(BF16)


def _deinterleave_up_weights(w1_all, layer):
    _, n_exp, d, f2 = w1_all.shape
    assert f2 % (2 * LANE) == 0
    tr = _divisor_tile(d, 256, 8)
    sel = np.zeros((2 * LANE, 2 * LANE), np.float32)
    sel[2 * np.arange(LANE), np.arange(LANE)] = 1.0
    sel[2 * np.arange(LANE) + 1, LANE + np.arange(LANE)] = 1.0
    return pl.pallas_call(
        _deinterleave_kernel, out_shape=jax.ShapeDtypeStruct((n_exp, d, f2), BF16),
        grid=(n_exp, d // tr),
        in_specs=[pl.BlockSpec((1, 1, tr, f2), lambda e, r: (layer, e, r, 0)),
                  pl.BlockSpec((2 * LANE, 2 * LANE), lambda e, r: (0, 0))],
        out_specs=pl.BlockSpec((1, tr, f2), lambda e, r: (e, r, 0)),
        compiler_params=_params("arbitrary", "arbitrary"), name="moe_weight_deinterleave",
    )(w1_all, jnp.asarray(sel, BF16))


DMA_UNROLL = 8


def _dispatch_kernel(dest_ref, src_ref, _, dst_ref, sem, *, tm, top_k):
    base = pl.program_id(0) * tm

    def issue(c, _):
        for u in range(DMA_UNROLL):
            t = c * DMA_UNROLL + u
            for k in range(top_k):
                d = dest_ref[(base + t) * top_k + k]
                pltpu.make_async_copy(src_ref.at[pl.ds(t, 1)], dst_ref.at[pl.ds(d, 1)], sem).start()
        return 0

    lax.fori_loop(0, tm // DMA_UNROLL, issue, 0)
    for _ in range(top_k):
        pltpu.make_async_copy(src_ref, dst_ref.at[pl.ds(0, tm)], sem).wait()


def _dispatch(dest, h, xg0, *, tm, top_k):
    n_tok, d = h.shape
    assert tm % DMA_UNROLL == 0 and h.dtype == F32
    return pl.pallas_call(
        functools.partial(_dispatch_kernel, tm=tm, top_k=top_k),
        out_shape=jax.ShapeDtypeStruct(xg0.shape, xg0.dtype),
        grid_spec=pltpu.PrefetchScalarGridSpec(
            num_scalar_prefetch=1, grid=(n_tok // tm,),
            in_specs=[pl.BlockSpec((tm, d), lambda i, dest: (i, 0)),
                      pl.BlockSpec(memory_space=pl.ANY)],
            out_specs=pl.BlockSpec(memory_space=pl.ANY),
            scratch_shapes=[pltpu.SemaphoreType.DMA]),
        input_output_aliases={2: 0},
        compiler_params=pltpu.CompilerParams(dimension_semantics=("arbitrary",),
                                             vmem_limit_bytes=VMEM_LIMIT, disable_bounds_checks=True),
        name="moe_dispatch",
    )(dest, h, xg0)


def _swiglu(x_glu, x_lin, alpha, limit):
    x_glu = jnp.minimum(x_glu, limit)
    x_lin = jnp.clip(x_lin, -limit, limit)
    return x_glu * jax.nn.sigmoid(alpha * x_glu) * (x_lin + 1.0)


def _moe_up_kernel(be_ref, nu_ref, x_ref, w_ref, b_ref, a_ref, *, tc, alpha, limit):
    i = pl.program_id(0)

    @pl.when(i < nu_ref[0])
    def _():
        x = x_ref[...].astype(BF16)
        for c in range(w_ref.shape[2] // tc):
            sl = slice(c * tc, (c + 1) * tc)
            hdn = jnp.dot(x, w_ref[0, :, sl], preferred_element_type=F32) + b_ref[0, :, sl]
            for j in range(tc // (2 * LANE)):
                g = hdn[:, 2 * j * LANE:(2 * j + 1) * LANE]
                lin = hdn[:, (2 * j + 1) * LANE:(2 * j + 2) * LANE]
                col = (c * tc // (2 * LANE) + j) * LANE
                a_ref[:, col:col + LANE] = _swiglu(g, lin, alpha, limit).astype(a_ref.dtype)

    @pl.when(i >= nu_ref[0])
    def _():
        a_ref[...] = jnp.zeros(a_ref.shape, a_ref.dtype)


def _moe_down_kernel(be_ref, nu_ref, a_ref, w_ref, b_ref, y_ref, w_bf16):
    i = pl.program_id(0)

    @pl.when((i == 0) | (be_ref[i] != be_ref[jnp.maximum(i - 1, 0)]))
    def _():
        w_bf16[...] = w_ref[0, 0].astype(BF16)

    @pl.when(i < nu_ref[0])
    def _():
        y_ref[...] = jnp.dot(a_ref[...], w_bf16[...], preferred_element_type=F32) + b_ref[0]

    @pl.when(i >= nu_ref[0])
    def _():
        y_ref[...] = jnp.zeros(y_ref.shape, y_ref.dtype)


def _moe_experts(blk_e, n_used, xg, w1, b1, w2_all, layer, b2, *, bm, cfg):
    d = xg.shape[1]
    n_exp, _, f2 = w1.shape
    ff = f2 // 2
    nb = blk_e.shape[0]
    p_rows = nb * bm
    tc = _divisor_tile(f2, 1024, 2 * LANE)
    a = pl.pallas_call(
        functools.partial(_moe_up_kernel, tc=tc, alpha=cfg.swiglu_alpha, limit=cfg.swiglu_limit),
        out_shape=jax.ShapeDtypeStruct((p_rows, ff), BF16),
        grid_spec=pltpu.PrefetchScalarGridSpec(
            num_scalar_prefetch=2, grid=(nb,),
            in_specs=[pl.BlockSpec((bm, d), lambda i, be, nu: (i, 0)),
                      pl.BlockSpec((1, d, f2), lambda i, be, nu: (be[i], 0, 0)),
                      pl.BlockSpec((1, 1, f2), lambda i, be, nu: (be[i], 0, 0))],
            out_specs=pl.BlockSpec((bm, ff), lambda i, be, nu: (i, 0))),
        compiler_params=_params("arbitrary"), name="moe_up_swiglu",
    )(blk_e, n_used, xg, w1, b1)
    return pl.pallas_call(
        _moe_down_kernel,
        out_shape=jax.ShapeDtypeStruct((p_rows, d), F32),
        grid_spec=pltpu.PrefetchScalarGridSpec(
            num_scalar_prefetch=2, grid=(nb,),
            in_specs=[pl.BlockSpec((bm, ff), lambda i, be, nu: (i, 0)),
                      pl.BlockSpec((1, 1, ff, d), lambda i, be, nu: (layer, be[i], 0, 0)),
                      pl.BlockSpec((1, 1, d), lambda i, be, nu: (be[i], 0, 0))],
            out_specs=pl.BlockSpec((bm, d), lambda i, be, nu: (i, 0)),
            scratch_shapes=[pltpu.VMEM((ff, d), BF16)]),
        compiler_params=_params("arbitrary"), name="moe_down",
    )(blk_e, n_used, a, w2_all, b2)


def _combine_kernel(dest_ref, y_ref, w_ref, o_ref, buf, sem, *, tm, top_k):
    base = pl.program_id(0) * tm

    def issue(c, _):
        for u in range(DMA_UNROLL):
            t = c * DMA_UNROLL + u
            for k in range(top_k):
                d = dest_ref[(base + t) * top_k + k]
                pltpu.make_async_copy(y_ref.at[pl.ds(d, 1)], buf.at[k, pl.ds(t, 1)], sem).start()
        return 0

    lax.fori_loop(0, tm // DMA_UNROLL, issue, 0)
    for k in range(top_k):
        pltpu.make_async_copy(y_ref.at[pl.ds(0, tm)], buf.at[k], sem).wait()
    w = w_ref[...]
    acc = buf[0] * w[:, 0:1]
    for k in range(1, top_k):
        acc = acc + buf[k] * w[:, k:k + 1]
    o_ref[...] = acc


def _combine(dest, y, top_w, *, tm, top_k):
    d = y.shape[1]
    n_tok, lanes = top_w.shape
    assert tm % DMA_UNROLL == 0
    return pl.pallas_call(
        functools.partial(_combine_kernel, tm=tm, top_k=top_k),
        out_shape=jax.ShapeDtypeStruct((n_tok, d), F32),
        grid_spec=pltpu.PrefetchScalarGridSpec(
            num_scalar_prefetch=1, grid=(n_tok // tm,),
            in_specs=[pl.BlockSpec(memory_space=pl.ANY),
                      pl.BlockSpec((tm, lanes), lambda i, dest: (i, 0))],
            out_specs=pl.BlockSpec((tm, d), lambda i, dest: (i, 0)),
            scratch_shapes=[pltpu.VMEM((top_k, tm, d), F32), pltpu.SemaphoreType.DMA]),
        compiler_params=pltpu.CompilerParams(dimension_semantics=("arbitrary",),
                                             vmem_limit_bytes=VMEM_LIMIT, disable_bounds_checks=True),
        name="moe_combine",
    )(dest, y, top_w)


def _moe(h, top_i, top_w, valid, w1, b1, w2_all, layer, b2, *, cfg):
    n_tok, d = h.shape
    n_exp = w1.shape[0]
    kk = cfg.top_k
    bm = cfg.moe_rows
    idx = top_i[:, :kk]
    sel = (idx[:, :, None] == jnp.arange(n_exp, dtype=jnp.int32)[None, None, :]) & valid[:, None, None]
    sel = jnp.any(sel, axis=1).astype(jnp.int32)
    counts = jnp.sum(sel, axis=0)
    rank = jnp.cumsum(sel, axis=0) - sel
    padded = (counts + bm - 1) // bm * bm
    gend = jnp.cumsum(padded)
    gstart = gend - padded
    onehot = (idx[:, :, None] == jnp.arange(n_exp, dtype=jnp.int32)[None, None, :]).astype(jnp.int32)
    dest = jnp.sum(onehot * (gstart[None, None, :] + rank[:, None, :]), axis=-1)
    n_blocks = -(-(n_tok * kk) // bm) + n_exp
    p_rows = n_blocks * bm
    starts = jnp.arange(n_blocks, dtype=jnp.int32) * bm
    blk_e = jnp.minimum(jnp.sum(gend[None, :] <= starts[:, None], axis=1), n_exp - 1).astype(jnp.int32)
    n_used = (gend[-1] // bm).astype(jnp.int32).reshape(1)

    tmd = _divisor_tile(n_tok, cfg.gather_rows, 16)
    spare = p_rows + jnp.arange(n_tok * kk, dtype=jnp.int32).reshape(n_tok, kk) % (tmd * kk)
    dest_w = jnp.where(valid[:, None], dest, spare).astype(jnp.int32).reshape(-1)
    xg = _dispatch(dest_w, h, jnp.zeros((p_rows + tmd * kk, d), h.dtype), tm=tmd, top_k=kk)
    y = _moe_experts(blk_e, n_used, xg, w1, b1, w2_all, layer, b2, bm=bm, cfg=cfg)
    dest_r = jnp.where(valid[:, None], dest, 0).astype(jnp.int32).reshape(-1)
    tmc = _divisor_tile(n_tok, cfg.combine_rows, 8)
    return _combine(dest_r, y, top_w, tm=tmc, top_k=kk)


def _final_kernel(x_ref, f_ref, gate_ref, g_ref, o_ref, *, eps):
    x = x_ref[...] + gate_ref[0] * f_ref[...]
    ms = jnp.mean(x * x, axis=-1, keepdims=True)
    o_ref[0] = x * lax.rsqrt(ms + eps) * g_ref[...]


def _final_norm(x, f, gate, g, *, batch, n_ctx, n_lat, eps):
    d = x.shape[1]
    ts = n_ctx
    tpb = (n_ctx + n_lat) // ts
    row = pl.BlockSpec((ts, d), lambda b, s: (b * tpb + 1 + s, 0))
    return pl.pallas_call(
        functools.partial(_final_kernel, eps=eps),
        out_shape=jax.ShapeDtypeStruct((batch, n_lat, d), F32),
        grid=(batch, n_lat // ts),
        in_specs=[row, row, pl.BlockSpec((1, 1, d), lambda b, s: (b, 0, 0)),
                  pl.BlockSpec((1, d), lambda b, s: (0, 0))],
        out_specs=pl.BlockSpec((1, ts, d), lambda b, s: (b, s, 0)),
        compiler_params=_params("arbitrary", "arbitrary"), name="final_norm",
    )(x, f, gate, g.reshape(1, d))


def _rope_tables(kind, n_ctx, n_lat, grid_w, theta, dims):
    half = dims // 2
    nf = half // 2
    freqs = theta ** (-np.arange(0, half, 2, dtype=np.float64) / half)
    t = np.arange(n_lat)
    pos_r = np.concatenate([np.zeros(n_ctx), t // grid_w]).astype(np.float64)
    pos_c = np.concatenate([np.zeros(n_ctx), t % grid_w]).astype(np.float64)
    ang_r = pos_r[:, None] * freqs[None, :]
    ang_c = pos_c[:, None] * freqs[None, :]
    ang = np.concatenate([ang_r, ang_c], axis=1)
    rows = n_ctx + n_lat
    cos = np.ones((rows, LANE), np.float32)
    sin = np.zeros((rows, LANE), np.float32)
    w = 2 * nf
    cos[:, :w] = np.cos(ang)
    cos[:, LANE // 2:LANE // 2 + w] = np.cos(ang)
    sin[:, :w] = -np.sin(ang)
    sin[:, LANE // 2:LANE // 2 + w] = np.sin(ang)
    del kind
    return jnp.asarray(cos), jnp.asarray(sin)


def _axial_lanes(w, dims):
    nf = dims // 4
    lead = w.shape[:-1]
    w = jnp.swapaxes(w.reshape(lead + (2, 2, nf)), -3, -2).reshape(lead + (2, 2 * nf))
    w = jnp.pad(w, [(0, 0)] * (len(lead) + 1) + [(0, LANE // 2 - 2 * nf)])
    return w.reshape(lead + (LANE,))


def _forward(x, c, ctx, c_ctx, w_mod, b_mod, g_mix, w_in, g_q_a, w_uq, g_kv_a, w_ukv, g_qn, g_kn,
             rpb, w_branch, w_out, g_ffn, w_router, b_router, w_exp1, b_exp1, w_exp2, b_exp2, g_final,
             cfg):
    batch, n_lat, d = x.shape
    n_ctx = ctx.shape[1]
    depth = w_mod.shape[0]
    t_rows = n_ctx + n_lat
    n_rows = batch * t_rows
    hd = cfg.head_dim
    q_rank, kv_rank = g_q_a.shape[1], g_kv_a.shape[1]
    ha, hb, hkv, hc = cfg.mla_heads, cfg.gqa_heads, cfg.gqa_kv_heads, cfg.na_heads
    nope, rdim, vdim = cfg.mla_nope, cfg.mla_rope, cfg.mla_v
    assert nope == LANE and vdim == LANE and hd == LANE and rdim <= LANE // 2
    assert n_lat % n_ctx == 0 and n_ctx % 16 == 0 and batch <= 7
    kh_full = (rpb.shape[2] + 1) // 2
    kw = (rpb.shape[3] + 1) // 2
    rows = n_lat // cfg.grid_w
    kh = min(kh_full, rows)
    n_branch = w_branch.shape[1]
    assert n_branch == 3

    tm = _divisor_tile(t_rows, cfg.row_tile, 16)
    tpb = t_rows // tm

    sizes = (q_rank, kv_rank, rdim, hb * hd, hkv * hd, hkv * hd, hc * hd, hc * hd, hc * hd, n_branch * d)
    offs = np.concatenate([[0], np.cumsum(sizes)])

    cos_a, sin_a = _rope_tables("mla", n_ctx, n_lat, cfg.grid_w, cfg.rope_theta, rdim)
    cos_b, sin_b = _rope_tables("gqa", n_ctx, n_lat, cfg.grid_w, cfg.rope_theta, hd)
    rope_spec = pl.BlockSpec((tm, LANE), lambda i, j: (i % tpb, 0))
    log2e = float(np.log2(np.e))
    mla_scale = float((nope + rdim) ** -0.5) * log2e
    head_scale = float(hd ** -0.5) * log2e

    cc = jnp.zeros((8, d), F32).at[:batch].set(c).at[batch].set(c_ctx)
    mod = _modulation(cc, w_mod, b_mod)

    def table(layer, chunks):
        rows_ = []
        for ch in chunks:
            sl = mod[layer, :, ch * d:(ch + 1) * d]
            rows_.append(jnp.broadcast_to(sl[batch][None], (batch, d)))
            rows_.append(sl[:batch])
        tab = jnp.stack(rows_, axis=1)
        return jnp.pad(tab, ((0, 0), (0, 8 - tab.shape[1]), (0, 0)))

    xa = jnp.concatenate([ctx, x], axis=1).reshape(n_rows, d)
    pend_f = None
    pend_tab = None
    is_lat = (jnp.arange(n_rows, dtype=jnp.int32) % t_rows) >= n_ctx

    for layer in range(depth):
        last = layer == depth - 1
        wi = w_in[layer]

        w_d = wi[:, offs[0]:offs[2]].astype(BF16)
        g_d = jnp.concatenate([g_q_a[layer], g_kv_a[layer]]).reshape(1, -1)
        w_kr = _axial_lanes(wi[:, offs[2]:offs[3]], rdim).astype(BF16)
        uq = w_uq[layer].reshape(q_rank, ha, nope + rdim)
        uq = jnp.concatenate([uq[:, :, :nope], _axial_lanes(uq[:, :, nope:], rdim)], axis=-1)
        uq = uq.reshape(q_rank, ha * 2 * LANE).astype(BF16)
        ukv = w_ukv[layer].reshape(kv_rank, ha, nope + vdim)
        uk = ukv[:, :, :nope].reshape(kv_rank, ha * nope).astype(BF16)
        uv = ukv[:, :, nope:].reshape(kv_rank, ha * vdim).astype(BF16)
        w_qb = _axial_lanes(wi[:, offs[3]:offs[4]].reshape(d, hb, hd), hd).reshape(d, hb * hd).astype(BF16)
        w_kb = _axial_lanes(wi[:, offs[4]:offs[5]].reshape(d, hkv, hd), hd).reshape(d, hkv * hd).astype(BF16)
        w_vb = wi[:, offs[5]:offs[6]].astype(BF16)
        g_qn_p = _axial_lanes(g_qn[layer], hd).reshape(1, hd)
        g_kn_p = _axial_lanes(g_kn[layer], hd).reshape(1, hd)
        w_c = wi[:, offs[6]:offs[9]].astype(BF16)
        c_scale = jnp.concatenate([jnp.full((hc * hd,), head_scale, F32),
                                   jnp.ones((2 * hc * hd,), F32)]).reshape(1, -1)
        w_g = wi[:, offs[9]:offs[10]].reshape(d, n_branch, d).transpose(1, 0, 2).astype(BF16)
        w_b = w_branch[layer].astype(BF16)
        w_o = w_out[layer].astype(BF16)
        ff = w_exp2.shape[2]
        w1 = _deinterleave_up_weights(w_exp1, layer)
        b1 = b_exp1[layer].reshape(-1, ff // LANE, LANE, 2).transpose(0, 1, 3, 2).reshape(-1, 1, 2 * ff)
        b2 = b_exp2[layer].reshape(-1, 1, d)

        tab1 = table(layer, (0, 1)) if pend_f is None else jnp.concatenate(
            [table(layer, (0, 1))[:, :4], pend_tab[:, :2], jnp.zeros((batch, 2, d), F32)], axis=1)
        res = _norm_mod(xa, pend_f, tab1, g_mix[layer], None, t_rows=t_rows, n_ctx=n_ctx, cfg=cfg)
        if pend_f is not None:
            xa, h = res
        else:
            (h,) = res

        dn = _matmul(h, w_d, functools.partial(_epi_rmsnorm, eps=cfg.eps),
                     [(g_d, pl.BlockSpec((1, q_rank), lambda i, j: (0, j)))],
                     jax.ShapeDtypeStruct((n_rows, q_rank + kv_rank), BF16),
                     pl.BlockSpec((tm, q_rank), lambda i, j: (i, j)), tm=tm, tn=q_rank, name="mla_down")
        assert q_rank == kv_rank
        k_pe = _matmul(h, w_kr,
                       functools.partial(_epi_rope_tiles, rope_tiles=(True,), norm=False, eps=cfg.eps, scale=1.0),
                       [(cos_a, rope_spec), (sin_a, rope_spec)],
                       jax.ShapeDtypeStruct((n_rows, LANE), BF16),
                       pl.BlockSpec((tm, LANE), lambda i, j: (i, 0)), tm=tm, tn=LANE, name="mla_rope_key")
        hpt = 2 if ha % 2 == 0 else 1
        q_a = _matmul(dn, uq,
                      functools.partial(_epi_rope_tiles, rope_tiles=(False, True), norm=False, eps=cfg.eps,
                                        scale=mla_scale),
                      [(cos_a, rope_spec), (sin_a, rope_spec)],
                      jax.ShapeDtypeStruct((n_rows, ha * 2 * LANE), BF16),
                      pl.BlockSpec((tm, hpt * 2 * LANE), lambda i, j: (i, j)), tm=tm, tn=hpt * 2 * LANE, a_col=0,
                      name="mla_q_up")
        k_a = _matmul(dn, uk, _epi_mla_key,
                      [(k_pe, pl.BlockSpec((tm, LANE), lambda i, j: (i, 0)))],
                      jax.ShapeDtypeStruct((n_rows, ha * 2 * LANE), BF16),
                      pl.BlockSpec((tm, hpt * 2 * LANE), lambda i, j: (i, j)), tm=tm, tn=hpt * LANE, a_col=1,
                      name="mla_k_up")
        tn_v = _divisor_tile(ha * vdim, 4 * LANE, LANE)
        v_a = _matmul(dn, uv, _epi_plain, [],
                      jax.ShapeDtypeStruct((n_rows, ha * vdim), BF16),
                      pl.BlockSpec((tm, tn_v), lambda i, j: (i, j)), tm=tm, tn=tn_v, a_col=1, name="mla_v_up")

        tn_q = _divisor_tile(hb * hd, 4 * LANE, LANE)
        q_b = _matmul(h, w_qb,
                      functools.partial(_epi_rope_tiles, rope_tiles=(True,), norm=True, eps=cfg.eps,
                                        scale=head_scale),
                      [(g_qn_p, pl.BlockSpec((1, hd), lambda i, j: (0, 0))), (cos_b, rope_spec), (sin_b, rope_spec)],
                      jax.ShapeDtypeStruct((n_rows, hb * hd), BF16),
                      pl.BlockSpec((tm, tn_q), lambda i, j: (i, j)), tm=tm, tn=tn_q, name="gqa_q")
        tn_k = _divisor_tile(hkv * hd, 4 * LANE, LANE)
        k_b = _matmul(h, w_kb,
                      functools.partial(_epi_rope_tiles, rope_tiles=(True,), norm=True, eps=cfg.eps, scale=1.0),
                      [(g_kn_p, pl.BlockSpec((1, hd), lambda i, j: (0, 0))), (cos_b, rope_spec), (sin_b, rope_spec)],
                      jax.ShapeDtypeStruct((n_rows, hkv * hd), BF16),
                      pl.BlockSpec((tm, tn_k), lambda i, j: (i, j)), tm=tm, tn=tn_k, name="gqa_k")
        v_b = _matmul(h, w_vb, _epi_plain, [],
                      jax.ShapeDtypeStruct((n_rows, hkv * hd), BF16),
                      pl.BlockSpec((tm, tn_k), lambda i, j: (i, j)), tm=tm, tn=tn_k, name="gqa_v")

        tn_c = _divisor_tile(3 * hc * hd, 4 * LANE, LANE)
        qkv_c = _matmul(h, w_c, _epi_scale, [(c_scale, pl.BlockSpec((1, tn_c), lambda i, j: (0, j)))],
                        jax.ShapeDtypeStruct((n_rows, 3 * hc * hd), BF16),
                        pl.BlockSpec((tm, tn_c), lambda i, j: (i, j)), tm=tm, tn=tn_c, name="natten_qkv")

        common = dict(batch=batch, n_ctx=n_ctx, n_lat=n_lat, cfg=cfg)
        assert ha * vdim == hb * hd == hc * hd
        o_a = o_b = o_c = jnp.zeros((n_rows, hc * hd), BF16)
        spec_a = dict(heads=ha, kv_heads=ha, dk=2 * LANE, dv=vdim, q_col=0, k_col=0, v_col=0)
        spec_b = dict(heads=hb, kv_heads=hkv, dk=hd, dv=hd, q_col=0, k_col=0, v_col=0)
        if not last:
            o_a = _attention(q_a, k_a, v_a, o_a, q_tiles="ctx", name="mla_ctx_attention", **spec_a, **common)
            o_b = _attention(q_b, k_b, v_b, o_b, q_tiles="ctx", name="gqa_ctx_attention", **spec_b, **common)
            o_c = _attention(qkv_c, qkv_c, qkv_c, o_c, heads=hc, kv_heads=hc, dk=hd, dv=hd,
                             q_col=0, k_col=hc, v_col=2 * hc, q_tiles="ctx", name="natten_ctx_attention",
                             **common)
        o_a = _attention(q_a, k_a, v_a, o_a, q_tiles="lat", name="mla_attention", **spec_a, **common)
        o_b = _attention(q_b, k_b, v_b, o_b, q_tiles="lat", name="gqa_attention", **spec_b, **common)
        bias, group_type = _natten_bias(rpb[layer], rows, cfg.grid_w, kh, kw, n_ctx // cfg.grid_w, log2e)
        o_c = _natten(qkv_c, o_c, bias, group_type, batch=batch, n_ctx=n_ctx, n_lat=n_lat, heads=hc, hd=hd,
                      grid_w=cfg.grid_w, kh=kh, name="natten")

        tn_m = _divisor_tile(d, 4 * LANE, LANE)
        merged = _merge(h, o_a, o_b, o_c, w_g, w_b, tm=tm, tn=tn_m)
        gate1 = table(layer, (2,))
        x1 = _matmul(merged, w_o, functools.partial(_epi_residual, tiles_per_batch=tpb, tm=tm, n_ctx=n_ctx),
                     [(xa, pl.BlockSpec((tm, tn_m), lambda i, j: (i, j))),
                      (gate1, pl.BlockSpec((1, 8, tn_m), lambda i, j: (i // tpb, 0, j)))],
                     jax.ShapeDtypeStruct((n_rows, d), F32),
                     pl.BlockSpec((tm, tn_m), lambda i, j: (i, j)), tm=tm, tn=tn_m, name="out_proj_residual")

        h2, top_i, top_w = _norm_mod(x1, None, table(layer, (3, 4)), g_ffn[layer],
                                     (w_router[layer], b_router[layer]), t_rows=t_rows, n_ctx=n_ctx, cfg=cfg)
        valid = is_lat if last else jnp.ones((n_rows,), bool)
        pend_f = _moe(h2, top_i, top_w, valid, w1, b1, w_exp2, layer, b2, cfg=cfg)
        pend_tab = table(layer, (5,))
        xa = x1

    gate2 = pend_tab[:, 1:2]
    return _final_norm(xa, pend_f, gate2, g_final, batch=batch, n_ctx=n_ctx, n_lat=n_lat, eps=cfg.eps)


def kernel(x, c, ctx, c_ctx, w_mod, b_mod, g_mix, w_in, g_q_a, w_uq, g_kv_a, w_ukv, g_qn, g_kn, rpb,
           w_branch, w_out, g_ffn, w_router, b_router, w_exp1, b_exp1, w_exp2, b_exp2, g_final):
    return _forward(x, c, ctx, c_ctx, w_mod, b_mod, g_mix, w_in, g_q_a, w_uq, g_kv_a, w_ukv, g_qn, g_kn,
                    rpb, w_branch, w_out, g_ffn, w_router, b_router, w_exp1, b_exp1, w_exp2, b_exp2,
                    g_final, CFG)
```

```python
import functools
from typing import NamedTuple

import numpy as np
import jax
import jax.numpy as jnp
from jax import lax
from jax.experimental import pallas as pl
from jax.experimental.pallas import tpu as pltpu

F32 = jnp.float32
BF16 = jnp.bfloat16

LANE = 128
VMEM_LIMIT = 56 * 1024 * 1024
NEG = -1e30


class Config(NamedTuple):
    grid_w: int = 64
    eps: float = 1e-6
    rope_theta: float = 10000.0
    mla_heads: int = 8
    mla_nope: int = 128
    mla_rope: int = 64
    mla_v: int = 128
    gqa_heads: int = 8
    gqa_kv_heads: int = 2
    head_dim: int = 128
    na_heads: int = 8
    top_k: int = 4
    swiglu_alpha: float = 1.702
    swiglu_limit: float = 7.0
    row_tile: int = 1100
    kv_tile: int = 1024
    attn_heads_per_step: int = 4
    moe_rows: int = 256
    dispatch_rows: int = 512
    combine_rows: int = 256


CFG = Config()


def _divisor_tile(n, target, mult):
    best = None
    for d in range(mult, min(n, target) + 1, mult):
        if n % d == 0:
            best = d
    assert best is not None, (n, target, mult)
    return best


def _params(*sem):
    return pltpu.CompilerParams(dimension_semantics=sem, vmem_limit_bytes=VMEM_LIMIT)


def _mod_kernel(c_ref, w_ref, b_ref, o_ref):
    c = c_ref[...]
    a = (c * jax.nn.sigmoid(c)).astype(BF16)
    o_ref[0] = jnp.dot(a, w_ref[0].astype(BF16), preferred_element_type=F32) + b_ref[0]


def _modulation(cc, w_mod, b_mod):
    depth, d, n6 = w_mod.shape
    tn = _divisor_tile(n6, 1024, LANE)
    return pl.pallas_call(
        _mod_kernel,
        out_shape=jax.ShapeDtypeStruct((depth, 8, n6), F32),
        grid=(depth, n6 // tn),
        in_specs=[pl.BlockSpec((8, d), lambda l, j: (0, 0)),
                  pl.BlockSpec((1, d, tn), lambda l, j: (l, 0, j)),
                  pl.BlockSpec((1, 1, tn), lambda l, j: (l, 0, j))],
        out_specs=pl.BlockSpec((1, 8, tn), lambda l, j: (l, 0, j)),
        compiler_params=_params("arbitrary", "arbitrary"),
        name="adaln_modulation",
    )(cc, w_mod, b_mod.reshape(depth, 1, n6))


def _ctx_mask(tile_idx, tiles_per_batch, tm, n_ctx):
    row = (tile_idx % tiles_per_batch) * tm + lax.broadcasted_iota(jnp.int32, (tm, 1), 0)
    return row < n_ctx


def _norm_kernel(*refs, has_res, has_router, tiles_per_batch, tm, n_ctx, eps, top_k):
    it = iter(refs)
    x_ref = next(it)
    f_ref = next(it) if has_res else None
    tab_ref = next(it)
    g_ref = next(it)
    if has_router:
        wr_ref = next(it)
        br_ref = next(it)
    xo_ref = next(it) if has_res else None
    h_ref = next(it)
    if has_router:
        ti_ref = next(it)
        tw_ref = next(it)

    is_ctx = _ctx_mask(pl.program_id(0), tiles_per_batch, tm, n_ctx)
    tab = tab_ref[0]

    def pick(k):
        return jnp.where(is_ctx, tab[2 * k:2 * k + 1], tab[2 * k + 1:2 * k + 2])

    x = x_ref[...]
    if has_res:
        x = x + pick(2) * f_ref[...]
        xo_ref[...] = x
    ms = jnp.mean(x * x, axis=-1, keepdims=True)
    y = x * lax.rsqrt(ms + eps) * g_ref[...]
    n = y * (1.0 + pick(1)) + pick(0)
    h_ref[...] = n.astype(h_ref.dtype)

    if has_router:
        logits = jnp.dot(n, wr_ref[...], preferred_element_type=F32,
                         precision=lax.Precision.HIGHEST) + br_ref[...]
        lane = lax.broadcasted_iota(jnp.int32, logits.shape, 1)
        vals, idxs = [], []
        cur = logits
        for _ in range(top_k):
            m = jnp.max(cur, axis=-1, keepdims=True)
            idx = jnp.min(jnp.where(cur == m, lane, LANE), axis=-1, keepdims=True)
            vals.append(m)
            idxs.append(idx)
            cur = jnp.where(lane == idx, NEG, cur)
        es = [jnp.exp(v - vals[0]) for v in vals]
        den = es[0]
        for e in es[1:]:
            den = den + e
        ti = jnp.zeros(logits.shape, jnp.int32)
        tw = jnp.zeros(logits.shape, F32)
        for k in range(top_k):
            ti = jnp.where(lane == k, idxs[k], ti)
            tw = jnp.where(lane == k, es[k] / den, tw)
        ti_ref[...] = ti
        tw_ref[...] = tw


def _norm_mod(x, f, tab, g, router, *, t_rows, n_ctx, cfg):
    n_rows, d = x.shape
    tm = _divisor_tile(t_rows, 256, 8)
    tpb = t_rows // tm
    has_res = f is not None
    has_router = router is not None
    row = pl.BlockSpec((tm, d), lambda i: (i, 0))
    ins = [x] + ([f] if has_res else []) + [tab, g.reshape(1, d)]
    in_specs = [row] + ([row] if has_res else []) + [
        pl.BlockSpec((1, 8, d), lambda i: (i // tpb, 0, 0)),
        pl.BlockSpec((1, d), lambda i: (0, 0))]
    outs, out_specs = [], []
    if has_res:
        outs.append(jax.ShapeDtypeStruct((n_rows, d), F32))
        out_specs.append(row)
    outs.append(jax.ShapeDtypeStruct((n_rows, d), F32 if has_router else BF16))
    out_specs.append(row)
    if has_router:
        w_r, b_r = router
        n_exp = w_r.shape[1]
        assert n_exp <= LANE
        wr = jnp.zeros((d, LANE), F32).at[:, :n_exp].set(w_r)
        br = jnp.full((1, LANE), NEG, F32).at[0, :n_exp].set(b_r)
        ins += [wr, br]
        in_specs += [pl.BlockSpec((d, LANE), lambda i: (0, 0)), pl.BlockSpec((1, LANE), lambda i: (0, 0))]
        lane_row = pl.BlockSpec((tm, LANE), lambda i: (i, 0))
        outs += [jax.ShapeDtypeStruct((n_rows, LANE), jnp.int32), jax.ShapeDtypeStruct((n_rows, LANE), F32)]
        out_specs += [lane_row, lane_row]
    res = pl.pallas_call(
        functools.partial(_norm_kernel, has_res=has_res, has_router=has_router, tiles_per_batch=tpb,
                          tm=tm, n_ctx=n_ctx, eps=cfg.eps, top_k=cfg.top_k),
        out_shape=outs, grid=(n_rows // tm,), in_specs=in_specs, out_specs=out_specs,
        compiler_params=_params("arbitrary"),
        name="norm_modulate_router" if has_router else "norm_modulate",
    )(*ins)
    return list(res)


def _rope(t, cos, sin):
    return t * cos + pltpu.roll(t, LANE // 2, 1) * sin


def _mm_kernel(*refs, epilogue, n_extra):
    a_ref, w_ref = refs[0], refs[1]
    extra = refs[2:2 + n_extra]
    outs = refs[2 + n_extra:]
    acc = jnp.dot(a_ref[...], w_ref[...], preferred_element_type=F32)
    epilogue(acc, extra, outs)


def _matmul(a, w, epilogue, extras, out_shape, out_spec, *, tm, tn, a_col=0, name):
    m = a.shape[0]
    k, nw = w.shape
    in_specs = [pl.BlockSpec((tm, k), lambda i, j: (i, a_col)), pl.BlockSpec((k, tn), lambda i, j: (0, j))]
    in_specs += [s for _, s in extras]
    return pl.pallas_call(
        functools.partial(_mm_kernel, epilogue=epilogue, n_extra=len(extras)),
        out_shape=out_shape, grid=(m // tm, nw // tn), in_specs=in_specs, out_specs=out_spec,
        compiler_params=_params("arbitrary", "arbitrary"), name=name,
    )(a, w, *[x for x, _ in extras])


def _epi_rmsnorm(acc, extra, outs, *, eps):
    (g_ref,), (o_ref,) = extra, outs
    ms = jnp.mean(acc * acc, axis=-1, keepdims=True)
    o_ref[...] = (acc * lax.rsqrt(ms + eps) * g_ref[...]).astype(o_ref.dtype)


def _epi_scale(acc, extra, outs):
    (s_ref,), (o_ref,) = extra, outs
    o_ref[...] = (acc * s_ref[...]).astype(o_ref.dtype)


def _epi_plain(acc, extra, outs):
    outs[0][...] = acc.astype(outs[0].dtype)


def _epi_rope_tiles(acc, extra, outs, *, rope_tiles, norm, eps, scale):
    if norm:
        g_ref, cos_ref, sin_ref = extra
    else:
        cos_ref, sin_ref = extra
    o_ref = outs[0]
    cos, sin = cos_ref[...], sin_ref[...]
    for t in range(acc.shape[1] // LANE):
        v = acc[:, t * LANE:(t + 1) * LANE]
        if norm:
            ms = jnp.mean(v * v, axis=-1, keepdims=True)
            v = v * lax.rsqrt(ms + eps) * g_ref[...]
        if rope_tiles[t % len(rope_tiles)]:
            v = _rope(v, cos, sin)
        o_ref[:, t * LANE:(t + 1) * LANE] = (v * scale).astype(o_ref.dtype)


def _epi_mla_key(acc, extra, outs):
    (kpe_ref,), (o_ref,) = extra, outs
    for hh in range(acc.shape[1] // LANE):
        o_ref[:, 2 * hh * LANE:(2 * hh + 1) * LANE] = acc[:, hh * LANE:(hh + 1) * LANE].astype(o_ref.dtype)
        o_ref[:, (2 * hh + 1) * LANE:(2 * hh + 2) * LANE] = kpe_ref[...]


def _epi_residual(acc, extra, outs, *, tiles_per_batch, tm, n_ctx):
    (x_ref, tab_ref), (o_ref,) = extra, outs
    is_ctx = _ctx_mask(pl.program_id(0), tiles_per_batch, tm, n_ctx)
    tab = tab_ref[0]
    gate = jnp.where(is_ctx, tab[0:1], tab[1:2])
    o_ref[...] = x_ref[...] + gate * acc


def _attn_kernel(q_ref, k_ref, v_ref, o_ref, *, n_ctx, n_steps, tk, hps, kvps, dk, dv):
    tq = q_ref.shape[0]
    assert dv == LANE
    qs = [q_ref[:, i * dk:(i + 1) * dk] for i in range(hps)]

    def step(start, size, state):
        ones = jnp.ones((size, LANE), BF16)
        new = []
        for i, (m, acc) in enumerate(state):
            kv = i * kvps // hps
            k = k_ref[pl.ds(start, size), kv * dk:(kv + 1) * dk]
            v = jnp.concatenate([v_ref[pl.ds(start, size), kv * dv:(kv + 1) * dv], ones], axis=1)
            s = lax.dot_general(qs[i], k, (((1,), (1,)), ((), ())), preferred_element_type=F32)
            m_new = jnp.maximum(m, jnp.max(s, axis=-1, keepdims=True))
            p = jnp.exp2(s - m_new).astype(BF16)
            acc = jnp.exp2(m - m_new) * acc + jnp.dot(p, v, preferred_element_type=F32)
            new.append((m_new, acc))
        return tuple(new)

    state = tuple((jnp.full((tq, 1), NEG, F32), jnp.zeros((tq, dv + LANE), F32)) for _ in range(hps))
    if n_steps == 0:
        state = step(0, n_ctx, state)
    else:
        state = step(0, n_ctx + tk, state)
        if n_steps <= 4:
            for c in range(1, n_steps):
                state = step(n_ctx + c * tk, tk, state)
        else:
            def body(c, state):
                return step(pl.multiple_of(n_ctx + c * tk, n_ctx), tk, state)

            state = lax.fori_loop(1, n_steps, body, state)
    for i, (_, acc) in enumerate(state):
        o_ref[:, i * dv:(i + 1) * dv] = (acc[:, :dv] / acc[:, dv:]).astype(o_ref.dtype)


def _attention(q, k, v, out, *, batch, n_ctx, n_lat, heads, kv_heads, dk, dv, q_col, k_col, v_col,
               q_tiles, cfg, name):
    t_rows = n_ctx + n_lat
    tq = n_ctx
    tpb = t_rows // tq
    q_off = 1 if q_tiles == "lat" else 0
    n_q = tpb - 1 if q_tiles == "lat" else 1
    group = heads // kv_heads
    tk = _divisor_tile(n_lat, cfg.kv_tile, tq)
    n_steps = n_lat // tk if q_tiles == "lat" else 0
    k_rows = t_rows if q_tiles == "lat" else n_ctx
    kpb = t_rows // k_rows
    hps = cfg.attn_heads_per_step
    assert heads % hps == 0 and q_col % hps == 0
    if group == 1:
        kvps = hps
        assert k_col % hps == 0 and v_col % hps == 0

        def kv_block(col, h):
            return col // hps + h
    else:
        kvps = 1
        assert group % hps == 0

        def kv_block(col, h):
            return col + (h * hps) // group

    def kern(q_ref, k_ref, v_ref, _, o_ref):
        _attn_kernel(q_ref, k_ref, v_ref, o_ref, n_ctx=n_ctx, n_steps=n_steps, tk=tk, hps=hps, kvps=kvps,
                     dk=dk, dv=dv)

    return pl.pallas_call(
        kern, out_shape=jax.ShapeDtypeStruct(out.shape, out.dtype),
        grid=(batch, heads // hps, n_q),
        in_specs=[pl.BlockSpec((tq, hps * dk), lambda b, h, i: (b * tpb + i + q_off, q_col // hps + h)),
                  pl.BlockSpec((k_rows, kvps * dk), lambda b, h, i: (b * kpb, kv_block(k_col, h))),
                  pl.BlockSpec((k_rows, kvps * dv), lambda b, h, i: (b * kpb, kv_block(v_col, h))),
                  pl.BlockSpec(memory_space=pl.ANY)],
        out_specs=pl.BlockSpec((tq, hps * dv), lambda b, h, i: (b * tpb + i + q_off, h)),
        input_output_aliases={3: 0},
        compiler_params=_params("arbitrary", "arbitrary", "arbitrary"), name=name,
    )(q, k, v, out)


def _natten_geometry(rows, kh, group):
    union = group + kh - 1
    assert rows % group == 0 and rows >= union
    base = np.clip(group * np.arange(rows // group) - kh // 2, 0, rows - union)
    return union, base


def _natten_kernel(type_ref, q_ref, k_ref, v_ref, bias_ref, o_ref, *, n_ctx, grid_w, rows, kh, group, hps, hd):
    del type_ref
    union, _ = _natten_geometry(rows, kh, group)
    base = jnp.clip(group * pl.program_id(2) - kh // 2, 0, rows - union)
    start = pl.multiple_of(n_ctx + base * grid_w, grid_w)
    dn = (((1,), (1,)), ((), ()))
    for i in range(hps):
        cols = slice(i * hd, (i + 1) * hd)
        q = q_ref[:, cols]
        k_loc = k_ref[pl.ds(start, union * grid_w), cols]
        v_loc = v_ref[pl.ds(start, union * grid_w), cols]
        s_loc = lax.dot_general(q, k_loc, dn, preferred_element_type=F32) + bias_ref[i, 0]
        s_ctx = lax.dot_general(q, k_ref[pl.ds(0, n_ctx), cols], dn, preferred_element_type=F32)
        m = jnp.maximum(jnp.max(s_loc, axis=-1, keepdims=True), jnp.max(s_ctx, axis=-1, keepdims=True))
        p_loc = jnp.exp2(s_loc - m)
        p_ctx = jnp.exp2(s_ctx - m)
        l = jnp.sum(p_loc, axis=-1, keepdims=True) + jnp.sum(p_ctx, axis=-1, keepdims=True)
        acc = jnp.dot(p_loc.astype(BF16), v_loc, preferred_element_type=F32)
        acc = acc + jnp.dot(p_ctx.astype(BF16), v_ref[pl.ds(0, n_ctx), cols], preferred_element_type=F32)
        o_ref[:, cols] = (acc / l).astype(o_ref.dtype)


def _natten_bias(rpb, rows, grid_w, kh, kw, group, scale):
    full_kh = (rpb.shape[1] + 1) // 2
    union, base = _natten_geometry(rows, kh, group)
    n_groups = rows // group
    r = group * np.arange(n_groups)[:, None] + np.arange(group)[None, :]
    r0 = np.clip(r - kh // 2, 0, rows - kh)
    key_row = base[:, None] + np.arange(union)[None, :]
    valid = (key_row[:, None, :] >= r0[:, :, None]) & (key_row[:, None, :] < r0[:, :, None] + kh)
    assert (valid.sum(-1) == kh).all()
    dr = np.where(valid, key_row[:, None, :] - r[:, :, None] + full_kh - 1, -1)
    patterns, type_of_group = np.unique(dr.reshape(n_groups, -1), axis=0, return_inverse=True)
    dr = patterns.reshape(-1, group, union)
    row_hot = (dr[..., None] == np.arange(2 * full_kh - 1)).astype(np.float32)
    col = np.arange(grid_w)
    col0 = np.clip(col - kw // 2, 0, grid_w - kw)
    inside = (col[None, :] >= col0[:, None]) & (col[None, :] < col0[:, None] + kw)
    dc = col[None, :] - col[:, None] + kw - 1
    col_hot = ((dc[..., None] == np.arange(2 * kw - 1)) & inside[..., None]).astype(np.float32)
    hi = lax.Precision.HIGHEST
    t = jnp.einsum("trus,hsv->htruv", jnp.asarray(row_hot), rpb.astype(F32), precision=hi)
    b = jnp.einsum("htruv,ckv->htrcuk", t, jnp.asarray(col_hot), precision=hi)
    mask = (dr >= 0)[None, :, :, None, :, None] & inside[None, None, None, :, None, :]
    b = jnp.where(jnp.asarray(mask), b * scale, NEG)
    b = b.reshape(rpb.shape[0], dr.shape[0], group * grid_w, union * grid_w)
    return b, jnp.asarray(type_of_group.reshape(-1), jnp.int32)


def _natten(qkv, out, bias, group_type, *, batch, n_ctx, n_lat, heads, hd, grid_w, kh, name):
    t_rows = n_ctx + n_lat
    rows = n_lat // grid_w
    assert n_ctx % grid_w == 0
    group = n_ctx // grid_w
    tpb = t_rows // n_ctx
    union, _ = _natten_geometry(rows, kh, group)
    hps = 2 if heads % 2 == 0 else 1
    hb = heads // hps

    def kern(type_ref, q_ref, k_ref, v_ref, bias_ref, _, o_ref):
        _natten_kernel(type_ref, q_ref, k_ref, v_ref, bias_ref, o_ref, n_ctx=n_ctx, grid_w=grid_w, rows=rows,
                       kh=kh, group=group, hps=hps, hd=hd)

    return pl.pallas_call(
        kern, out_shape=jax.ShapeDtypeStruct(out.shape, out.dtype),
        grid_spec=pltpu.PrefetchScalarGridSpec(
            num_scalar_prefetch=1, grid=(batch, hb, rows // group),
            in_specs=[pl.BlockSpec((n_ctx, hps * hd), lambda b, h, g, ty: (b * tpb + 1 + g, h)),
                      pl.BlockSpec((t_rows, hps * hd), lambda b, h, g, ty: (b, hb + h)),
                      pl.BlockSpec((t_rows, hps * hd), lambda b, h, g, ty: (b, 2 * hb + h)),
                      pl.BlockSpec((hps, 1, n_ctx, union * grid_w), lambda b, h, g, ty: (h, ty[g], 0, 0)),
                      pl.BlockSpec(memory_space=pl.ANY)],
            out_specs=pl.BlockSpec((n_ctx, hps * hd), lambda b, h, g, ty: (b * tpb + 1 + g, h))),
        input_output_aliases={5: 0},
        compiler_params=_params("arbitrary", "arbitrary", "arbitrary"), name=name,
    )(group_type, qkv, qkv, qkv, bias, out)


def _merge_kernel(h_ref, oa_ref, ob_ref, oc_ref, wg_ref, wb_ref, o_ref):
    h = h_ref[...]
    acc = None
    for i, o in enumerate((oa_ref, ob_ref, oc_ref)):
        gate = jax.nn.sigmoid(jnp.dot(h, wg_ref[i], preferred_element_type=F32))
        term = gate * jnp.dot(o[...], wb_ref[i], preferred_element_type=F32)
        acc = term if acc is None else acc + term
    o_ref[...] = acc.astype(o_ref.dtype)


def _merge(h, o_a, o_b, o_c, wg, wb, *, tm, tn):
    n_rows, d = h.shape
    bw = o_a.shape[1]
    nb = wg.shape[0]
    return pl.pallas_call(
        _merge_kernel, out_shape=jax.ShapeDtypeStruct((n_rows, d), BF16),
        grid=(n_rows // tm, d // tn),
        in_specs=[pl.BlockSpec((tm, d), lambda i, j: (i, 0))]
        + [pl.BlockSpec((tm, bw), lambda i, j: (i, 0))] * 3
        + [pl.BlockSpec((nb, d, tn), lambda i, j: (0, 0, j)),
           pl.BlockSpec((nb, bw, tn), lambda i, j: (0, 0, j))],
        out_specs=pl.BlockSpec((tm, tn), lambda i, j: (i, j)),
        compiler_params=_params("arbitrary", "arbitrary"), name="gated_merge",
    )(h, o_a, o_b, o_c, wg, wb)


def _deinterleave_kernel(w_ref, sel_ref, o_ref):
    sel = sel_ref[...]
    for c in range(w_ref.shape[3] // (2 * LANE)):
        sl = slice(c * 2 * LANE, (c + 1) * 2 * LANE)
        blk = w_ref[0, 0, :, sl].astype(BF16)
        o_ref[0, :, sl] = jnp.dot(blk, sel, preferred_element_type=F32).astype(BF16)


def _deinterleave_up_weights(w1_all, layer):
    _, n_exp, d, f2 = w1_all.shape
    assert f2 % (2 * LANE) == 0
    tr = _divisor_tile(d, 256, 8)
    sel = np.zeros((2 * LANE, 2 * LANE), np.float32)
    sel[2 * np.arange(LANE), np.arange(LANE)] = 1.0
    sel[2 * np.arange(LANE) + 1, LANE + np.arange(LANE)] = 1.0
    return pl.pallas_call(
        _deinterleave_kernel, out_shape=jax.ShapeDtypeStruct((n_exp, d, f2), BF16),
        grid=(n_exp, d // tr),
        in_specs=[pl.BlockSpec((1, 1, tr, f2), lambda e, r: (layer, e, r, 0)),
                  pl.BlockSpec((2 * LANE, 2 * LANE), lambda e, r: (0, 0))],
        out_specs=pl.BlockSpec((1, tr, f2), lambda e, r: (e, r, 0)),
        compiler_params=_params("arbitrary", "arbitrary"), name="moe_weight_deinterleave",
    )(w1_all, jnp.asarray(sel, BF16))


DMA_UNROLL = 8


def _dispatch_kernel(dest_ref, src_ref, _, dst_ref, sem, *, tm, top_k):
    base = pl.program_id(0) * tm

    def issue(c, _):
        for u in range(DMA_UNROLL):
            t = c * DMA_UNROLL + u
            for k in range(top_k):
                d = dest_ref[(base + t) * top_k + k]
                pltpu.make_async_copy(src_ref.at[pl.ds(t, 1)], dst_ref.at[pl.ds(d, 1)], sem).start()
        return 0

    lax.fori_loop(0, tm // DMA_UNROLL, issue, 0)
    for _ in range(top_k):
        pltpu.make_async_copy(src_ref, dst_ref.at[pl.ds(0, tm)], sem).wait()


def _dispatch(dest, h, xg0, *, tm, top_k):
    n_tok, d = h.shape
    assert tm % DMA_UNROLL == 0 and h.dtype == F32
    return pl.pallas_call(
        functools.partial(_dispatch_kernel, tm=tm, top_k=top_k),
        out_shape=jax.ShapeDtypeStruct(xg0.shape, xg0.dtype),
        grid_spec=pltpu.PrefetchScalarGridSpec(
            num_scalar_prefetch=1, grid=(n_tok // tm,),
            in_specs=[pl.BlockSpec((tm, d), lambda i, dest: (i, 0)),
                      pl.BlockSpec(memory_space=pl.ANY)],
            out_specs=pl.BlockSpec(memory_space=pl.ANY),
            scratch_shapes=[pltpu.SemaphoreType.DMA]),
        input_output_aliases={2: 0},
        compiler_params=pltpu.CompilerParams(dimension_semantics=("arbitrary",),
                                             vmem_limit_bytes=VMEM_LIMIT, disable_bounds_checks=True),
        name="moe_dispatch",
    )(dest, h, xg0)


def _swiglu(x_glu, x_lin, alpha, limit):
    x_glu = jnp.minimum(x_glu, limit)
    x_lin = jnp.clip(x_lin, -limit, limit)
    return x_glu * jax.nn.sigmoid(alpha * x_glu) * (x_lin + 1.0)


def _moe_up_kernel(be_ref, nu_ref, x_ref, w_ref, b_ref, a_ref, *, tc, alpha, limit):
    i = pl.program_id(0)

    @pl.when(i < nu_ref[0])
    def _():
        x = x_ref[...].astype(BF16)
        for c in range(w_ref.shape[2] // tc):
            sl = slice(c * tc, (c + 1) * tc)
            hdn = jnp.dot(x, w_ref[0, :, sl], preferred_element_type=F32) + b_ref[0, :, sl]
            for j in range(tc // (2 * LANE)):
                g = hdn[:, 2 * j * LANE:(2 * j + 1) * LANE]
                lin = hdn[:, (2 * j + 1) * LANE:(2 * j + 2) * LANE]
                col = (c * tc // (2 * LANE) + j) * LANE
                a_ref[:, col:col + LANE] = _swiglu(g, lin, alpha, limit).astype(a_ref.dtype)

    @pl.when(i >= nu_ref[0])
    def _():
        a_ref[...] = jnp.zeros(a_ref.shape, a_ref.dtype)


def _moe_down_kernel(be_ref, nu_ref, a_ref, w_ref, b_ref, y_ref, w_bf16):
    i = pl.program_id(0)

    @pl.when((i == 0) | (be_ref[i] != be_ref[jnp.maximum(i - 1, 0)]))
    def _():
        w_bf16[...] = w_ref[0, 0].astype(BF16)

    @pl.when(i < nu_ref[0])
    def _():
        y_ref[...] = jnp.dot(a_ref[...], w_bf16[...], preferred_element_type=F32) + b_ref[0]

    @pl.when(i >= nu_ref[0])
    def _():
        y_ref[...] = jnp.zeros(y_ref.shape, y_ref.dtype)


def _moe_experts(blk_e, n_used, xg, w1, b1, w2_all, layer, b2, *, bm, cfg):
    d = xg.shape[1]
    n_exp, _, f2 = w1.shape
    ff = f2 // 2
    nb = blk_e.shape[0]
    p_rows = nb * bm
    tc = _divisor_tile(f2, 1024, 2 * LANE)
    a = pl.pallas_call(
        functools.partial(_moe_up_kernel, tc=tc, alpha=cfg.swiglu_alpha, limit=cfg.swiglu_limit),
        out_shape=jax.ShapeDtypeStruct((p_rows, ff), BF16),
        grid_spec=pltpu.PrefetchScalarGridSpec(
            num_scalar_prefetch=2, grid=(nb,),
            in_specs=[pl.BlockSpec((bm, d), lambda i, be, nu: (i, 0)),
                      pl.BlockSpec((1, d, f2), lambda i, be, nu: (be[i], 0, 0)),
                      pl.BlockSpec((1, 1, f2), lambda i, be, nu: (be[i], 0, 0))],
            out_specs=pl.BlockSpec((bm, ff), lambda i, be, nu: (i, 0))),
        compiler_params=_params("arbitrary"), name="moe_up_swiglu",
    )(blk_e, n_used, xg, w1, b1)
    return pl.pallas_call(
        _moe_down_kernel,
        out_shape=jax.ShapeDtypeStruct((p_rows, d), F32),
        grid_spec=pltpu.PrefetchScalarGridSpec(
            num_scalar_prefetch=2, grid=(nb,),
            in_specs=[pl.BlockSpec((bm, ff), lambda i, be, nu: (i, 0)),
                      pl.BlockSpec((1, 1, ff, d), lambda i, be, nu: (layer, be[i], 0, 0)),
                      pl.BlockSpec((1, 1, d), lambda i, be, nu: (be[i], 0, 0))],
            out_specs=pl.BlockSpec((bm, d), lambda i, be, nu: (i, 0)),
            scratch_shapes=[pltpu.VMEM((ff, d), BF16)]),
        compiler_params=_params("arbitrary"), name="moe_down",
    )(blk_e, n_used, a, w2_all, b2)


def _combine_kernel(dest_ref, y_ref, w_ref, o_ref, buf, sem, *, tm, top_k):
    base = pl.program_id(0) * tm

    def issue(c, _):
        for u in range(DMA_UNROLL):
            t = c * DMA_UNROLL + u
            for k in range(top_k):
                d = dest_ref[(base + t) * top_k + k]
                pltpu.make_async_copy(y_ref.at[pl.ds(d, 1)], buf.at[k, pl.ds(t, 1)], sem).start()
        return 0

    lax.fori_loop(0, tm // DMA_UNROLL, issue, 0)
    for k in range(top_k):
        pltpu.make_async_copy(y_ref.at[pl.ds(0, tm)], buf.at[k], sem).wait()
    w = w_ref[...]
    acc = buf[0] * w[:, 0:1]
    for k in range(1, top_k):
        acc = acc + buf[k] * w[:, k:k + 1]
    o_ref[...] = acc


def _combine(dest, y, top_w, *, tm, top_k):
    d = y.shape[1]
    n_tok, lanes = top_w.shape
    assert tm % DMA_UNROLL == 0
    return pl.pallas_call(
        functools.partial(_combine_kernel, tm=tm, top_k=top_k),
        out_shape=jax.ShapeDtypeStruct((n_tok, d), F32),
        grid_spec=pltpu.PrefetchScalarGridSpec(
            num_scalar_prefetch=1, grid=(n_tok // tm,),
            in_specs=[pl.BlockSpec(memory_space=pl.ANY),
                      pl.BlockSpec((tm, lanes), lambda i, dest: (i, 0))],
            out_specs=pl.BlockSpec((tm, d), lambda i, dest: (i, 0)),
            scratch_shapes=[pltpu.VMEM((top_k, tm, d), F32), pltpu.SemaphoreType.DMA]),
        compiler_params=pltpu.CompilerParams(dimension_semantics=("arbitrary",),
                                             vmem_limit_bytes=VMEM_LIMIT, disable_bounds_checks=True),
        name="moe_combine",
    )(dest, y, top_w)


def _moe(h, top_i, top_w, valid, w1, b1, w2_all, layer, b2, *, cfg):
    n_tok, d = h.shape
    n_exp = w1.shape[0]
    kk = cfg.top_k
    bm = cfg.moe_rows
    idx = top_i[:, :kk]
    sel = (idx[:, :, None] == jnp.arange(n_exp, dtype=jnp.int32)[None, None, :]) & valid[:, None, None]
    sel = jnp.any(sel, axis=1).astype(jnp.int32)
    counts = jnp.sum(sel, axis=0)
    rank = jnp.cumsum(sel, axis=0) - sel
    padded = (counts + bm - 1) // bm * bm
    gend = jnp.cumsum(padded)
    gstart = gend - padded
    onehot = (idx[:, :, None] == jnp.arange(n_exp, dtype=jnp.int32)[None, None, :]).astype(jnp.int32)
    dest = jnp.sum(onehot * (gstart[None, None, :] + rank[:, None, :]), axis=-1)
    n_blocks = -(-(n_tok * kk) // bm) + n_exp
    p_rows = n_blocks * bm
    starts = jnp.arange(n_blocks, dtype=jnp.int32) * bm
    blk_e = jnp.minimum(jnp.sum(gend[None, :] <= starts[:, None], axis=1), n_exp - 1).astype(jnp.int32)
    n_used = (gend[-1] // bm).astype(jnp.int32).reshape(1)

    tmd = _divisor_tile(n_tok, cfg.dispatch_rows, 16)
    spare = p_rows + jnp.arange(n_tok * kk, dtype=jnp.int32).reshape(n_tok, kk) % (tmd * kk)
    dest_w = jnp.where(valid[:, None], dest, spare).astype(jnp.int32).reshape(-1)
    xg = _dispatch(dest_w, h, jnp.zeros((p_rows + tmd * kk, d), h.dtype), tm=tmd, top_k=kk)
    y = _moe_experts(blk_e, n_used, xg, w1, b1, w2_all, layer, b2, bm=bm, cfg=cfg)
    any_row = jnp.arange(n_tok * kk, dtype=jnp.int32).reshape(n_tok, kk) % p_rows
    dest_r = jnp.where(valid[:, None], dest, any_row).astype(jnp.int32).reshape(-1)
    tmc = _divisor_tile(n_tok, cfg.combine_rows, 8)
    return _combine(dest_r, y, top_w, tm=tmc, top_k=kk)


def _final_kernel(x_ref, f_ref, gate_ref, g_ref, o_ref, *, eps):
    x = x_ref[...] + gate_ref[0] * f_ref[...]
    ms = jnp.mean(x * x, axis=-1, keepdims=True)
    o_ref[0] = x * lax.rsqrt(ms + eps) * g_ref[...]


def _final_norm(x, f, gate, g, *, batch, n_ctx, n_lat, eps):
    d = x.shape[1]
    ts = n_ctx
    tpb = (n_ctx + n_lat) // ts
    row = pl.BlockSpec((ts, d), lambda b, s: (b * tpb + 1 + s, 0))
    return pl.pallas_call(
        functools.partial(_final_kernel, eps=eps),
        out_shape=jax.ShapeDtypeStruct((batch, n_lat, d), F32),
        grid=(batch, n_lat // ts),
        in_specs=[row, row, pl.BlockSpec((1, 1, d), lambda b, s: (b, 0, 0)),
                  pl.BlockSpec((1, d), lambda b, s: (0, 0))],
        out_specs=pl.BlockSpec((1, ts, d), lambda b, s: (b, s, 0)),
        compiler_params=_params("arbitrary", "arbitrary"), name="final_norm",
    )(x, f, gate, g.reshape(1, d))


def _rope_tables(n_ctx, n_lat, grid_w, theta, dims):
    half = dims // 2
    nf = half // 2
    freqs = theta ** (-np.arange(0, half, 2, dtype=np.float64) / half)
    t = np.arange(n_lat)
    pos_r = np.concatenate([np.zeros(n_ctx), t // grid_w]).astype(np.float64)
    pos_c = np.concatenate([np.zeros(n_ctx), t % grid_w]).astype(np.float64)
    ang_r = pos_r[:, None] * freqs[None, :]
    ang_c = pos_c[:, None] * freqs[None, :]
    ang = np.concatenate([ang_r, ang_c], axis=1)
    rows = n_ctx + n_lat
    cos = np.ones((rows, LANE), np.float32)
    sin = np.zeros((rows, LANE), np.float32)
    w = 2 * nf
    cos[:, :w] = np.cos(ang)
    cos[:, LANE // 2:LANE // 2 + w] = np.cos(ang)
    sin[:, :w] = -np.sin(ang)
    sin[:, LANE // 2:LANE // 2 + w] = np.sin(ang)
    return jnp.asarray(cos), jnp.asarray(sin)


def _axial_lanes(w, dims):
    nf = dims // 4
    lead = w.shape[:-1]
    w = jnp.swapaxes(w.reshape(lead + (2, 2, nf)), -3, -2).reshape(lead + (2, 2 * nf))
    w = jnp.pad(w, [(0, 0)] * (len(lead) + 1) + [(0, LANE // 2 - 2 * nf)])
    return w.reshape(lead + (LANE,))


def _forward(x, c, ctx, c_ctx, w_mod, b_mod, g_mix, w_in, g_q_a, w_uq, g_kv_a, w_ukv, g_qn, g_kn,
             rpb, w_branch, w_out, g_ffn, w_router, b_router, w_exp1, b_exp1, w_exp2, b_exp2, g_final,
             cfg):
    batch, n_lat, d = x.shape
    n_ctx = ctx.shape[1]
    depth = w_mod.shape[0]
    t_rows = n_ctx + n_lat
    n_rows = batch * t_rows
    hd = cfg.head_dim
    q_rank, kv_rank = g_q_a.shape[1], g_kv_a.shape[1]
    ha, hb, hkv, hc = cfg.mla_heads, cfg.gqa_heads, cfg.gqa_kv_heads, cfg.na_heads
    nope, rdim, vdim = cfg.mla_nope, cfg.mla_rope, cfg.mla_v
    assert nope == LANE and vdim == LANE and hd == LANE and rdim <= LANE // 2
    assert n_lat % n_ctx == 0 and n_ctx % 16 == 0 and batch <= 7
    kh_full = (rpb.shape[2] + 1) // 2
    kw = (rpb.shape[3] + 1) // 2
    rows = n_lat // cfg.grid_w
    kh = min(kh_full, rows)
    n_branch = w_branch.shape[1]
    assert n_branch == 3

    tm = _divisor_tile(t_rows, cfg.row_tile, 16)
    tpb = t_rows // tm

    sizes = (q_rank, kv_rank, rdim, hb * hd, hkv * hd, hkv * hd, hc * hd, hc * hd, hc * hd, n_branch * d)
    offs = np.concatenate([[0], np.cumsum(sizes)])

    cos_a, sin_a = _rope_tables(n_ctx, n_lat, cfg.grid_w, cfg.rope_theta, rdim)
    cos_b, sin_b = _rope_tables(n_ctx, n_lat, cfg.grid_w, cfg.rope_theta, hd)
    rope_spec = pl.BlockSpec((tm, LANE), lambda i, j: (i % tpb, 0))
    log2e = float(np.log2(np.e))
    mla_scale = float((nope + rdim) ** -0.5) * log2e
    head_scale = float(hd ** -0.5) * log2e

    cc = jnp.zeros((8, d), F32).at[:batch].set(c).at[batch].set(c_ctx)
    mod = _modulation(cc, w_mod, b_mod)

    def table(layer, chunks):
        rows_ = []
        for ch in chunks:
            sl = mod[layer, :, ch * d:(ch + 1) * d]
            rows_.append(jnp.broadcast_to(sl[batch][None], (batch, d)))
            rows_.append(sl[:batch])
        tab = jnp.stack(rows_, axis=1)
        return jnp.pad(tab, ((0, 0), (0, 8 - tab.shape[1]), (0, 0)))

    xa = jnp.concatenate([ctx, x], axis=1).reshape(n_rows, d)
    pend_f = None
    pend_tab = None
    is_lat = (jnp.arange(n_rows, dtype=jnp.int32) % t_rows) >= n_ctx

    for layer in range(depth):
        last = layer == depth - 1
        wi = w_in[layer]

        w_d = wi[:, offs[0]:offs[2]].astype(BF16)
        g_d = jnp.concatenate([g_q_a[layer], g_kv_a[layer]]).reshape(1, -1)
        w_kr = _axial_lanes(wi[:, offs[2]:offs[3]], rdim).astype(BF16)
        uq = w_uq[layer].reshape(q_rank, ha, nope + rdim)
        uq = jnp.concatenate([uq[:, :, :nope], _axial_lanes(uq[:, :, nope:], rdim)], axis=-1)
        uq = uq.reshape(q_rank, ha * 2 * LANE).astype(BF16)
        ukv = w_ukv[layer].reshape(kv_rank, ha, nope + vdim)
        uk = ukv[:, :, :nope].reshape(kv_rank, ha * nope).astype(BF16)
        uv = ukv[:, :, nope:].reshape(kv_rank, ha * vdim).astype(BF16)
        w_qb = _axial_lanes(wi[:, offs[3]:offs[4]].reshape(d, hb, hd), hd).reshape(d, hb * hd).astype(BF16)
        w_kb = _axial_lanes(wi[:, offs[4]:offs[5]].reshape(d, hkv, hd), hd).reshape(d, hkv * hd).astype(BF16)
        w_vb = wi[:, offs[5]:offs[6]].astype(BF16)
        g_qn_p = _axial_lanes(g_qn[layer], hd).reshape(1, hd)
        g_kn_p = _axial_lanes(g_kn[layer], hd).reshape(1, hd)
        w_c = wi[:, offs[6]:offs[9]].astype(BF16)
        c_scale = jnp.concatenate([jnp.full((hc * hd,), head_scale, F32),
                                   jnp.ones((2 * hc * hd,), F32)]).reshape(1, -1)
        w_g = wi[:, offs[9]:offs[10]].reshape(d, n_branch, d).transpose(1, 0, 2).astype(BF16)
        w_b = w_branch[layer].astype(BF16)
        w_o = w_out[layer].astype(BF16)
        ff = w_exp2.shape[2]
        w1 = _deinterleave_up_weights(w_exp1, layer)
        b1 = b_exp1[layer].reshape(-1, ff // LANE, LANE, 2).transpose(0, 1, 3, 2).reshape(-1, 1, 2 * ff)
        b2 = b_exp2[layer].reshape(-1, 1, d)

        tab1 = table(layer, (0, 1)) if pend_f is None else jnp.concatenate(
            [table(layer, (0, 1))[:, :4], pend_tab[:, :2], jnp.zeros((batch, 2, d), F32)], axis=1)
        res = _norm_mod(xa, pend_f, tab1, g_mix[layer], None, t_rows=t_rows, n_ctx=n_ctx, cfg=cfg)
        if pend_f is not None:
            xa, h = res
        else:
            (h,) = res

        dn = _matmul(h, w_d, functools.partial(_epi_rmsnorm, eps=cfg.eps),
                     [(g_d, pl.BlockSpec((1, q_rank), lambda i, j: (0, j)))],
                     jax.ShapeDtypeStruct((n_rows, q_rank + kv_rank), BF16),
                     pl.BlockSpec((tm, q_rank), lambda i, j: (i, j)), tm=tm, tn=q_rank, name="mla_down")
        assert q_rank == kv_rank
        k_pe = _matmul(h, w_kr,
                       functools.partial(_epi_rope_tiles, rope_tiles=(True,), norm=False, eps=cfg.eps, scale=1.0),
                       [(cos_a, rope_spec), (sin_a, rope_spec)],
                       jax.ShapeDtypeStruct((n_rows, LANE), BF16),
                       pl.BlockSpec((tm, LANE), lambda i, j: (i, 0)), tm=tm, tn=LANE, name="mla_rope_key")
        hpt = 2 if ha % 2 == 0 else 1
        q_a = _matmul(dn, uq,
                      functools.partial(_epi_rope_tiles, rope_tiles=(False, True), norm=False, eps=cfg.eps,
                                        scale=mla_scale),
                      [(cos_a, rope_spec), (sin_a, rope_spec)],
                      jax.ShapeDtypeStruct((n_rows, ha * 2 * LANE), BF16),
                      pl.BlockSpec((tm, hpt * 2 * LANE), lambda i, j: (i, j)), tm=tm, tn=hpt * 2 * LANE, a_col=0,
                      name="mla_q_up")
        k_a = _matmul(dn, uk, _epi_mla_key,
                      [(k_pe, pl.BlockSpec((tm, LANE), lambda i, j: (i, 0)))],
                      jax.ShapeDtypeStruct((n_rows, ha * 2 * LANE), BF16),
                      pl.BlockSpec((tm, hpt * 2 * LANE), lambda i, j: (i, j)), tm=tm, tn=hpt * LANE, a_col=1,
                      name="mla_k_up")
        tn_v = _divisor_tile(ha * vdim, 4 * LANE, LANE)
        v_a = _matmul(dn, uv, _epi_plain, [],
                      jax.ShapeDtypeStruct((n_rows, ha * vdim), BF16),
                      pl.BlockSpec((tm, tn_v), lambda i, j: (i, j)), tm=tm, tn=tn_v, a_col=1, name="mla_v_up")

        tn_q = _divisor_tile(hb * hd, 4 * LANE, LANE)
        q_b = _matmul(h, w_qb,
                      functools.partial(_epi_rope_tiles, rope_tiles=(True,), norm=True, eps=cfg.eps,
                                        scale=head_scale),
                      [(g_qn_p, pl.BlockSpec((1, hd), lambda i, j: (0, 0))), (cos_b, rope_spec), (sin_b, rope_spec)],
                      jax.ShapeDtypeStruct((n_rows, hb * hd), BF16),
                      pl.BlockSpec((tm, tn_q), lambda i, j: (i, j)), tm=tm, tn=tn_q, name="gqa_q")
        tn_k = _divisor_tile(hkv * hd, 4 * LANE, LANE)
        k_b = _matmul(h, w_kb,
                      functools.partial(_epi_rope_tiles, rope_tiles=(True,), norm=True, eps=cfg.eps, scale=1.0),
                      [(g_kn_p, pl.BlockSpec((1, hd), lambda i, j: (0, 0))), (cos_b, rope_spec), (sin_b, rope_spec)],
                      jax.ShapeDtypeStruct((n_rows, hkv * hd), BF16),
                      pl.BlockSpec((tm, tn_k), lambda i, j: (i, j)), tm=tm, tn=tn_k, name="gqa_k")
        v_b = _matmul(h, w_vb, _epi_plain, [],
                      jax.ShapeDtypeStruct((n_rows, hkv * hd), BF16),
                      pl.BlockSpec((tm, tn_k), lambda i, j: (i, j)), tm=tm, tn=tn_k, name="gqa_v")

        tn_c = _divisor_tile(3 * hc * hd, 4 * LANE, LANE)
        qkv_c = _matmul(h, w_c, _epi_scale, [(c_scale, pl.BlockSpec((1, tn_c), lambda i, j: (0, j)))],
                        jax.ShapeDtypeStruct((n_rows, 3 * hc * hd), BF16),
                        pl.BlockSpec((tm, tn_c), lambda i, j: (i, j)), tm=tm, tn=tn_c, name="natten_qkv")

        common = dict(batch=batch, n_ctx=n_ctx, n_lat=n_lat, cfg=cfg)
        assert ha * vdim == hb * hd == hc * hd
        o_a = o_b = o_c = jnp.zeros((n_rows, hc * hd), BF16)
        spec_a = dict(heads=ha, kv_heads=ha, dk=2 * LANE, dv=vdim, q_col=0, k_col=0, v_col=0)
        spec_b = dict(heads=hb, kv_heads=hkv, dk=hd, dv=hd, q_col=0, k_col=0, v_col=0)
        if not last:
            o_a = _attention(q_a, k_a, v_a, o_a, q_tiles="ctx", name="mla_ctx_attention", **spec_a, **common)
            o_b = _attention(q_b, k_b, v_b, o_b, q_tiles="ctx", name="gqa_ctx_attention", **spec_b, **common)
            o_c = _attention(qkv_c, qkv_c, qkv_c, o_c, heads=hc, kv_heads=hc, dk=hd, dv=hd,
                             q_col=0, k_col=hc, v_col=2 * hc, q_tiles="ctx", name="natten_ctx_attention",
                             **common)
        o_a = _attention(q_a, k_a, v_a, o_a, q_tiles="lat", name="mla_attention", **spec_a, **common)
        o_b = _attention(q_b, k_b, v_b, o_b, q_tiles="lat", name="gqa_attention", **spec_b, **common)
        bias, group_type = _natten_bias(rpb[layer], rows, cfg.grid_w, kh, kw, n_ctx // cfg.grid_w, log2e)
        o_c = _natten(qkv_c, o_c, bias, group_type, batch=batch, n_ctx=n_ctx, n_lat=n_lat, heads=hc, hd=hd,
                      grid_w=cfg.grid_w, kh=kh, name="natten")

        tn_m = _divisor_tile(d, 4 * LANE, LANE)
        merged = _merge(h, o_a, o_b, o_c, w_g, w_b, tm=tm, tn=tn_m)
        gate1 = table(layer, (2,))
        x1 = _matmul(merged, w_o, functools.partial(_epi_residual, tiles_per_batch=tpb, tm=tm, n_ctx=n_ctx),
                     [(xa, pl.BlockSpec((tm, tn_m), lambda i, j: (i, j))),
                      (gate1, pl.BlockSpec((1, 8, tn_m), lambda i, j: (i // tpb, 0, j)))],
                     jax.ShapeDtypeStruct((n_rows, d), F32),
                     pl.BlockSpec((tm, tn_m), lambda i, j: (i, j)), tm=tm, tn=tn_m, name="out_proj_residual")

        h2, top_i, top_w = _norm_mod(x1, None, table(layer, (3, 4)), g_ffn[layer],
                                     (w_router[layer], b_router[layer]), t_rows=t_rows, n_ctx=n_ctx, cfg=cfg)
        valid = is_lat if last else jnp.ones((n_rows,), bool)
        pend_f = _moe(h2, top_i, top_w, valid, w1, b1, w_exp2, layer, b2, cfg=cfg)
        pend_tab = table(layer, (5,))
        xa = x1

    gate2 = pend_tab[:, 1:2]
    return _final_norm(xa, pend_f, gate2, g_final, batch=batch, n_ctx=n_ctx, n_lat=n_lat, eps=cfg.eps)


def kernel(x, c, ctx, c_ctx, w_mod, b_mod, g_mix, w_in, g_q_a, w_uq, g_kv_a, w_ukv, g_qn, g_kn, rpb,
           w_branch, w_out, g_ffn, w_router, b_router, w_exp1, b_exp1, w_exp2, b_exp2, g_final):
    return _forward(x, c, ctx, c_ctx, w_mod, b_mod, g_mix, w_in, g_q_a, w_uq, g_kv_a, w_ukv, g_qn, g_kn,
                    rpb, w_branch, w_out, g_ffn, w_router, b_router, w_exp1, b_exp1, w_exp2, b_exp2,
                    g_final, CFG)
```

```python
import functools
from typing import NamedTuple

import numpy as np
import jax
import jax.numpy as jnp
from jax import lax
from jax.experimental import pallas as pl
from jax.experimental.pallas import tpu as pltpu

F32 = jnp.float32
BF16 = jnp.bfloat16

LANE = 128
VMEM_LIMIT = 56 * 1024 * 1024
NEG = -1e30


class Config(NamedTuple):
    grid_w: int = 64
    eps: float = 1e-6
    rope_theta: float = 10000.0
    mla_heads: int = 8
    mla_nope: int = 128
    mla_rope: int = 64
    mla_v: int = 128
    gqa_heads: int = 8
    gqa_kv_heads: int = 2
    head_dim: int = 128
    na_heads: int = 8
    top_k: int = 4
    swiglu_alpha: float = 1.702
    swiglu_limit: float = 7.0
    row_tile: int = 1100
    kv_tile: int = 1024
    attn_heads_per_step: int = 4
    moe_rows: int = 256
    dispatch_rows: int = 1024
    combine_rows: int = 512


CFG = Config()


def _divisor_tile(n, target, mult):
    best = None
    for d in range(mult, min(n, target) + 1, mult):
        if n % d == 0:
            best = d
    assert best is not None, (n, target, mult)
    return best


def _params(*sem):
    return pltpu.CompilerParams(dimension_semantics=sem, vmem_limit_bytes=VMEM_LIMIT)


def _mod_kernel(c_ref, w_ref, b_ref, o_ref):
    c = c_ref[...]
    a = (c * jax.nn.sigmoid(c)).astype(BF16)
    o_ref[0] = jnp.dot(a, w_ref[0].astype(BF16), preferred_element_type=F32) + b_ref[0]


def _modulation(cc, w_mod, b_mod):
    depth, d, n6 = w_mod.shape
    tn = _divisor_tile(n6, 1024, LANE)
    return pl.pallas_call(
        _mod_kernel,
        out_shape=jax.ShapeDtypeStruct((depth, 8, n6), F32),
        grid=(depth, n6 // tn),
        in_specs=[pl.BlockSpec((8, d), lambda l, j: (0, 0)),
                  pl.BlockSpec((1, d, tn), lambda l, j: (l, 0, j)),
                  pl.BlockSpec((1, 1, tn), lambda l, j: (l, 0, j))],
        out_specs=pl.BlockSpec((1, 8, tn), lambda l, j: (l, 0, j)),
        compiler_params=_params("arbitrary", "arbitrary"),
        name="adaln_modulation",
    )(cc, w_mod, b_mod.reshape(depth, 1, n6))


def _ctx_mask(tile_idx, tiles_per_batch, tm, n_ctx):
    row = (tile_idx % tiles_per_batch) * tm + lax.broadcasted_iota(jnp.int32, (tm, 1), 0)
    return row < n_ctx


def _norm_kernel(*refs, has_res, has_router, tiles_per_batch, tm, n_ctx, eps, top_k):
    it = iter(refs)
    x_ref = next(it)
    f_ref = next(it) if has_res else None
    tab_ref = next(it)
    g_ref = next(it)
    if has_router:
        wr_ref = next(it)
        br_ref = next(it)
    xo_ref = next(it) if has_res else None
    h_ref = next(it)
    if has_router:
        ti_ref = next(it)
        tw_ref = next(it)

    is_ctx = _ctx_mask(pl.program_id(0), tiles_per_batch, tm, n_ctx)
    tab = tab_ref[0]

    def pick(k):
        return jnp.where(is_ctx, tab[2 * k:2 * k + 1], tab[2 * k + 1:2 * k + 2])

    x = x_ref[...]
    if has_res:
        x = x + pick(2) * f_ref[...]
        xo_ref[...] = x
    ms = jnp.mean(x * x, axis=-1, keepdims=True)
    y = x * lax.rsqrt(ms + eps) * g_ref[...]
    n = y * (1.0 + pick(1)) + pick(0)
    h_ref[...] = n.astype(h_ref.dtype)

    if has_router:
        logits = jnp.dot(n, wr_ref[...], preferred_element_type=F32,
                         precision=lax.Precision.HIGHEST) + br_ref[...]
        lane = lax.broadcasted_iota(jnp.int32, logits.shape, 1)
        vals, idxs = [], []
        cur = logits
        for _ in range(top_k):
            m = jnp.max(cur, axis=-1, keepdims=True)
            idx = jnp.min(jnp.where(cur == m, lane, LANE), axis=-1, keepdims=True)
            vals.append(m)
            idxs.append(idx)
            cur = jnp.where(lane == idx, NEG, cur)
        es = [jnp.exp(v - vals[0]) for v in vals]
        den = es[0]
        for e in es[1:]:
            den = den + e
        ti = jnp.zeros(logits.shape, jnp.int32)
        tw = jnp.zeros(logits.shape, F32)
        for k in range(top_k):
            ti = jnp.where(lane == k, idxs[k], ti)
            tw = jnp.where(lane == k, es[k] / den, tw)
        ti_ref[...] = ti
        tw_ref[...] = tw


def _norm_mod(x, f, tab, g, router, *, t_rows, n_ctx, cfg):
    n_rows, d = x.shape
    tm = _divisor_tile(t_rows, 256, 8)
    tpb = t_rows // tm
    has_res = f is not None
    has_router = router is not None
    row = pl.BlockSpec((tm, d), lambda i: (i, 0))
    ins = [x] + ([f] if has_res else []) + [tab, g.reshape(1, d)]
    in_specs = [row] + ([row] if has_res else []) + [
        pl.BlockSpec((1, 8, d), lambda i: (i // tpb, 0, 0)),
        pl.BlockSpec((1, d), lambda i: (0, 0))]
    outs, out_specs = [], []
    if has_res:
        outs.append(jax.ShapeDtypeStruct((n_rows, d), F32))
        out_specs.append(row)
    outs.append(jax.ShapeDtypeStruct((n_rows, d), F32 if has_router else BF16))
    out_specs.append(row)
    if has_router:
        w_r, b_r = router
        n_exp = w_r.shape[1]
        assert n_exp <= LANE
        wr = jnp.zeros((d, LANE), F32).at[:, :n_exp].set(w_r)
        br = jnp.full((1, LANE), NEG, F32).at[0, :n_exp].set(b_r)
        ins += [wr, br]
        in_specs += [pl.BlockSpec((d, LANE), lambda i: (0, 0)), pl.BlockSpec((1, LANE), lambda i: (0, 0))]
        lane_row = pl.BlockSpec((tm, LANE), lambda i: (i, 0))
        outs += [jax.ShapeDtypeStruct((n_rows, LANE), jnp.int32), jax.ShapeDtypeStruct((n_rows, LANE), F32)]
        out_specs += [lane_row, lane_row]
    res = pl.pallas_call(
        functools.partial(_norm_kernel, has_res=has_res, has_router=has_router, tiles_per_batch=tpb,
                          tm=tm, n_ctx=n_ctx, eps=cfg.eps, top_k=cfg.top_k),
        out_shape=outs, grid=(n_rows // tm,), in_specs=in_specs, out_specs=out_specs,
        compiler_params=_params("arbitrary"),
        name="norm_modulate_router" if has_router else "norm_modulate",
    )(*ins)
    return list(res)


def _rope(t, cos, sin):
    return t * cos + pltpu.roll(t, LANE // 2, 1) * sin


def _mm_kernel(*refs, epilogue, n_extra):
    a_ref, w_ref = refs[0], refs[1]
    extra = refs[2:2 + n_extra]
    outs = refs[2 + n_extra:]
    acc = jnp.dot(a_ref[...], w_ref[...], preferred_element_type=F32)
    epilogue(acc, extra, outs)


def _matmul(a, w, epilogue, extras, out_shape, out_spec, *, tm, tn, a_col=0, name):
    m = a.shape[0]
    k, nw = w.shape
    in_specs = [pl.BlockSpec((tm, k), lambda i, j: (i, a_col)), pl.BlockSpec((k, tn), lambda i, j: (0, j))]
    in_specs += [s for _, s in extras]
    return pl.pallas_call(
        functools.partial(_mm_kernel, epilogue=epilogue, n_extra=len(extras)),
        out_shape=out_shape, grid=(m // tm, nw // tn), in_specs=in_specs, out_specs=out_spec,
        compiler_params=_params("arbitrary", "arbitrary"), name=name,
    )(a, w, *[x for x, _ in extras])


def _epi_rmsnorm(acc, extra, outs, *, eps):
    (g_ref,), (o_ref,) = extra, outs
    ms = jnp.mean(acc * acc, axis=-1, keepdims=True)
    o_ref[...] = (acc * lax.rsqrt(ms + eps) * g_ref[...]).astype(o_ref.dtype)


def _epi_scale(acc, extra, outs):
    (s_ref,), (o_ref,) = extra, outs
    o_ref[...] = (acc * s_ref[...]).astype(o_ref.dtype)


def _epi_plain(acc, extra, outs):
    outs[0][...] = acc.astype(outs[0].dtype)


def _epi_rope_tiles(acc, extra, outs, *, rope_tiles, norm, eps, scale):
    if norm:
        g_ref, cos_ref, sin_ref = extra
    else:
        cos_ref, sin_ref = extra
    o_ref = outs[0]
    cos, sin = cos_ref[...], sin_ref[...]
    for t in range(acc.shape[1] // LANE):
        v = acc[:, t * LANE:(t + 1) * LANE]
        if norm:
            ms = jnp.mean(v * v, axis=-1, keepdims=True)
            v = v * lax.rsqrt(ms + eps) * g_ref[...]
        if rope_tiles[t % len(rope_tiles)]:
            v = _rope(v, cos, sin)
        o_ref[:, t * LANE:(t + 1) * LANE] = (v * scale).astype(o_ref.dtype)


def _epi_mla_key(acc, extra, outs):
    (kpe_ref,), (o_ref,) = extra, outs
    for hh in range(acc.shape[1] // LANE):
        o_ref[:, 2 * hh * LANE:(2 * hh + 1) * LANE] = acc[:, hh * LANE:(hh + 1) * LANE].astype(o_ref.dtype)
        o_ref[:, (2 * hh + 1) * LANE:(2 * hh + 2) * LANE] = kpe_ref[...]


def _epi_residual(acc, extra, outs, *, tiles_per_batch, tm, n_ctx):
    (x_ref, tab_ref), (o_ref,) = extra, outs
    is_ctx = _ctx_mask(pl.program_id(0), tiles_per_batch, tm, n_ctx)
    tab = tab_ref[0]
    gate = jnp.where(is_ctx, tab[0:1], tab[1:2])
    o_ref[...] = x_ref[...] + gate * acc


def _attn_kernel(q_ref, k_ref, v_ref, o_ref, *, n_ctx, n_steps, tk, hps, kvps, dk, dv):
    tq = q_ref.shape[0]
    assert dv == LANE
    qs = [q_ref[:, i * dk:(i + 1) * dk] for i in range(hps)]

    def step(start, size, state):
        ones = jnp.ones((size, LANE), BF16)
        new = []
        for i, (m, acc) in enumerate(state):
            kv = i * kvps // hps
            k = k_ref[pl.ds(start, size), kv * dk:(kv + 1) * dk]
            v = jnp.concatenate([v_ref[pl.ds(start, size), kv * dv:(kv + 1) * dv], ones], axis=1)
            s = lax.dot_general(qs[i], k, (((1,), (1,)), ((), ())), preferred_element_type=F32)
            m_new = jnp.maximum(m, jnp.max(s, axis=-1, keepdims=True))
            p = jnp.exp2(s - m_new).astype(BF16)
            acc = jnp.exp2(m - m_new) * acc + jnp.dot(p, v, preferred_element_type=F32)
            new.append((m_new, acc))
        return tuple(new)

    state = tuple((jnp.full((tq, 1), NEG, F32), jnp.zeros((tq, dv + LANE), F32)) for _ in range(hps))
    if n_steps == 0:
        state = step(0, n_ctx, state)
    else:
        state = step(0, n_ctx + tk, state)
        if n_steps <= 4:
            for c in range(1, n_steps):
                state = step(n_ctx + c * tk, tk, state)
        else:
            def body(c, state):
                return step(pl.multiple_of(n_ctx + c * tk, n_ctx), tk, state)

            state = lax.fori_loop(1, n_steps, body, state)
    for i, (_, acc) in enumerate(state):
        o_ref[:, i * dv:(i + 1) * dv] = (acc[:, :dv] / acc[:, dv:]).astype(o_ref.dtype)


def _attention(q, k, v, out, *, batch, n_ctx, n_lat, heads, kv_heads, dk, dv, q_col, k_col, v_col,
               q_tiles, cfg, name):
    t_rows = n_ctx + n_lat
    tq = n_ctx
    tpb = t_rows // tq
    q_off = 1 if q_tiles == "lat" else 0
    n_q = tpb - 1 if q_tiles == "lat" else 1
    group = heads // kv_heads
    tk = _divisor_tile(n_lat, cfg.kv_tile, tq)
    n_steps = n_lat // tk if q_tiles == "lat" else 0
    k_rows = t_rows if q_tiles == "lat" else n_ctx
    kpb = t_rows // k_rows
    hps = cfg.attn_heads_per_step
    assert heads % hps == 0 and q_col % hps == 0
    if group == 1:
        kvps = hps
        assert k_col % hps == 0 and v_col % hps == 0

        def kv_block(col, h):
            return col // hps + h
    else:
        kvps = 1
        assert group % hps == 0

        def kv_block(col, h):
            return col + (h * hps) // group

    def kern(q_ref, k_ref, v_ref, _, o_ref):
        _attn_kernel(q_ref, k_ref, v_ref, o_ref, n_ctx=n_ctx, n_steps=n_steps, tk=tk, hps=hps, kvps=kvps,
                     dk=dk, dv=dv)

    return pl.pallas_call(
        kern, out_shape=jax.ShapeDtypeStruct(out.shape, out.dtype),
        grid=(batch, heads // hps, n_q),
        in_specs=[pl.BlockSpec((tq, hps * dk), lambda b, h, i: (b * tpb + i + q_off, q_col // hps + h)),
                  pl.BlockSpec((k_rows, kvps * dk), lambda b, h, i: (b * kpb, kv_block(k_col, h))),
                  pl.BlockSpec((k_rows, kvps * dv), lambda b, h, i: (b * kpb, kv_block(v_col, h))),
                  pl.BlockSpec(memory_space=pl.ANY)],
        out_specs=pl.BlockSpec((tq, hps * dv), lambda b, h, i: (b * tpb + i + q_off, h)),
        input_output_aliases={3: 0},
        compiler_params=_params("arbitrary", "arbitrary", "arbitrary"), name=name,
    )(q, k, v, out)


def _natten_geometry(rows, kh, group):
    union = group + kh - 1
    assert rows % group == 0 and rows >= union
    base = np.clip(group * np.arange(rows // group) - kh // 2, 0, rows - union)
    return union, base


def _natten_kernel(type_ref, q_ref, k_ref, v_ref, bias_ref, o_ref, *, n_ctx, grid_w, rows, kh, group, hps, hd):
    del type_ref
    union, _ = _natten_geometry(rows, kh, group)
    base = jnp.clip(group * pl.program_id(2) - kh // 2, 0, rows - union)
    start = pl.multiple_of(n_ctx + base * grid_w, grid_w)
    dn = (((1,), (1,)), ((), ()))
    for i in range(hps):
        cols = slice(i * hd, (i + 1) * hd)
        q = q_ref[:, cols]
        k_loc = k_ref[pl.ds(start, union * grid_w), cols]
        v_loc = v_ref[pl.ds(start, union * grid_w), cols]
        s_loc = lax.dot_general(q, k_loc, dn, preferred_element_type=F32) + bias_ref[i, 0]
        s_ctx = lax.dot_general(q, k_ref[pl.ds(0, n_ctx), cols], dn, preferred_element_type=F32)
        m = jnp.maximum(jnp.max(s_loc, axis=-1, keepdims=True), jnp.max(s_ctx, axis=-1, keepdims=True))
        p_loc = jnp.exp2(s_loc - m)
        p_ctx = jnp.exp2(s_ctx - m)
        l = jnp.sum(p_loc, axis=-1, keepdims=True) + jnp.sum(p_ctx, axis=-1, keepdims=True)
        acc = jnp.dot(p_loc.astype(BF16), v_loc, preferred_element_type=F32)
        acc = acc + jnp.dot(p_ctx.astype(BF16), v_ref[pl.ds(0, n_ctx), cols], preferred_element_type=F32)
        o_ref[:, cols] = (acc / l).astype(o_ref.dtype)


def _natten_bias(rpb, rows, grid_w, kh, kw, group, scale):
    full_kh = (rpb.shape[1] + 1) // 2
    union, base = _natten_geometry(rows, kh, group)
    n_groups = rows // group
    r = group * np.arange(n_groups)[:, None] + np.arange(group)[None, :]
    r0 = np.clip(r - kh // 2, 0, rows - kh)
    key_row = base[:, None] + np.arange(union)[None, :]
    valid = (key_row[:, None, :] >= r0[:, :, None]) & (key_row[:, None, :] < r0[:, :, None] + kh)
    assert (valid.sum(-1) == kh).all()
    dr = np.where(valid, key_row[:, None, :] - r[:, :, None] + full_kh - 1, -1)
    patterns, type_of_group = np.unique(dr.reshape(n_groups, -1), axis=0, return_inverse=True)
    dr = patterns.reshape(-1, group, union)
    row_hot = (dr[..., None] == np.arange(2 * full_kh - 1)).astype(np.float32)
    col = np.arange(grid_w)
    col0 = np.clip(col - kw // 2, 0, grid_w - kw)
    inside = (col[None, :] >= col0[:, None]) & (col[None, :] < col0[:, None] + kw)
    dc = col[None, :] - col[:, None] + kw - 1
    col_hot = ((dc[..., None] == np.arange(2 * kw - 1)) & inside[..., None]).astype(np.float32)
    hi = lax.Precision.HIGHEST
    t = jnp.einsum("trus,hsv->htruv", jnp.asarray(row_hot), rpb.astype(F32), precision=hi)
    b = jnp.einsum("htruv,ckv->htrcuk", t, jnp.asarray(col_hot), precision=hi)
    mask = (dr >= 0)[None, :, :, None, :, None] & inside[None, None, None, :, None, :]
    b = jnp.where(jnp.asarray(mask), b * scale, NEG)
    b = b.reshape(rpb.shape[0], dr.shape[0], group * grid_w, union * grid_w)
    return b, jnp.asarray(type_of_group.reshape(-1), jnp.int32)


def _natten(qkv, out, bias, group_type, *, batch, n_ctx, n_lat, heads, hd, grid_w, kh, name):
    t_rows = n_ctx + n_lat
    rows = n_lat // grid_w
    assert n_ctx % grid_w == 0
    group = n_ctx // grid_w
    tpb = t_rows // n_ctx
    union, _ = _natten_geometry(rows, kh, group)
    hps = 2 if heads % 2 == 0 else 1
    hb = heads // hps

    def kern(type_ref, q_ref, k_ref, v_ref, bias_ref, _, o_ref):
        _natten_kernel(type_ref, q_ref, k_ref, v_ref, bias_ref, o_ref, n_ctx=n_ctx, grid_w=grid_w, rows=rows,
                       kh=kh, group=group, hps=hps, hd=hd)

    return pl.pallas_call(
        kern, out_shape=jax.ShapeDtypeStruct(out.shape, out.dtype),
        grid_spec=pltpu.PrefetchScalarGridSpec(
            num_scalar_prefetch=1, grid=(batch, hb, rows // group),
            in_specs=[pl.BlockSpec((n_ctx, hps * hd), lambda b, h, g, ty: (b * tpb + 1 + g, h)),
                      pl.BlockSpec((t_rows, hps * hd), lambda b, h, g, ty: (b, hb + h)),
                      pl.BlockSpec((t_rows, hps * hd), lambda b, h, g, ty: (b, 2 * hb + h)),
                      pl.BlockSpec((hps, 1, n_ctx, union * grid_w), lambda b, h, g, ty: (h, ty[g], 0, 0)),
                      pl.BlockSpec(memory_space=pl.ANY)],
            out_specs=pl.BlockSpec((n_ctx, hps * hd), lambda b, h, g, ty: (b * tpb + 1 + g, h))),
        input_output_aliases={5: 0},
        compiler_params=_params("arbitrary", "arbitrary", "arbitrary"), name=name,
    )(group_type, qkv, qkv, qkv, bias, out)


def _merge_kernel(h_ref, oa_ref, ob_ref, oc_ref, wg_ref, wb_ref, o_ref):
    h = h_ref[...]
    acc = None
    for i, o in enumerate((oa_ref, ob_ref, oc_ref)):
        gate = jax.nn.sigmoid(jnp.dot(h, wg_ref[i], preferred_element_type=F32))
        term = gate * jnp.dot(o[...], wb_ref[i], preferred_element_type=F32)
        acc = term if acc is None else acc + term
    o_ref[...] = acc.astype(o_ref.dtype)


def _merge(h, o_a, o_b, o_c, wg, wb, *, tm, tn):
    n_rows, d = h.shape
    bw = o_a.shape[1]
    nb = wg.shape[0]
    return pl.pallas_call(
        _merge_kernel, out_shape=jax.ShapeDtypeStruct((n_rows, d), BF16),
        grid=(n_rows // tm, d // tn),
        in_specs=[pl.BlockSpec((tm, d), lambda i, j: (i, 0))]
        + [pl.BlockSpec((tm, bw), lambda i, j: (i, 0))] * 3
        + [pl.BlockSpec((nb, d, tn), lambda i, j: (0, 0, j)),
           pl.BlockSpec((nb, bw, tn), lambda i, j: (0, 0, j))],
        out_specs=pl.BlockSpec((tm, tn), lambda i, j: (i, j)),
        compiler_params=_params("arbitrary", "arbitrary"), name="gated_merge",
    )(h, o_a, o_b, o_c, wg, wb)


def _deinterleave_kernel(w_ref, sel_ref, o_ref):
    sel = sel_ref[...]
    for c in range(w_ref.shape[3] // (2 * LANE)):
        sl = slice(c * 2 * LANE, (c + 1) * 2 * LANE)
        blk = w_ref[0, 0, :, sl].astype(BF16)
        o_ref[0, :, sl] = jnp.dot(blk, sel, preferred_element_type=F32).astype(BF16)


def _deinterleave_up_weights(w1_all, layer):
    _, n_exp, d, f2 = w1_all.shape
    assert f2 % (2 * LANE) == 0
    tr = _divisor_tile(d, 256, 8)
    sel = np.zeros((2 * LANE, 2 * LANE), np.float32)
    sel[2 * np.arange(LANE), np.arange(LANE)] = 1.0
    sel[2 * np.arange(LANE) + 1, LANE + np.arange(LANE)] = 1.0
    return pl.pallas_call(
        _deinterleave_kernel, out_shape=jax.ShapeDtypeStruct((n_exp, d, f2), BF16),
        grid=(n_exp, d // tr),
        in_specs=[pl.BlockSpec((1, 1, tr, f2), lambda e, r: (layer, e, r, 0)),
                  pl.BlockSpec((2 * LANE, 2 * LANE), lambda e, r: (0, 0))],
        out_specs=pl.BlockSpec((1, tr, f2), lambda e, r: (e, r, 0)),
        compiler_params=_params("arbitrary", "arbitrary"), name="moe_weight_deinterleave",
    )(w1_all, jnp.asarray(sel, BF16))


DMA_UNROLL = 8


def _dispatch_kernel(dest_ref, src_ref, _, dst_ref, sem, *, tm, top_k):
    base = pl.program_id(0) * tm

    def issue(c, _):
        for u in range(DMA_UNROLL):
            t = c * DMA_UNROLL + u
            for k in range(top_k):
                d = dest_ref[(base + t) * top_k + k]
                pltpu.make_async_copy(src_ref.at[pl.ds(t, 1)], dst_ref.at[pl.ds(d, 1)], sem).start()
        return 0

    lax.fori_loop(0, tm // DMA_UNROLL, issue, 0)
    for _ in range(top_k):
        pltpu.make_async_copy(src_ref, dst_ref.at[pl.ds(0, tm)], sem).wait()


def _dispatch(dest, h, xg0, *, tm, top_k):
    n_tok, d = h.shape
    assert tm % DMA_UNROLL == 0 and h.dtype == F32
    return pl.pallas_call(
        functools.partial(_dispatch_kernel, tm=tm, top_k=top_k),
        out_shape=jax.ShapeDtypeStruct(xg0.shape, xg0.dtype),
        grid_spec=pltpu.PrefetchScalarGridSpec(
            num_scalar_prefetch=1, grid=(n_tok // tm,),
            in_specs=[pl.BlockSpec((tm, d), lambda i, dest: (i, 0)),
                      pl.BlockSpec(memory_space=pl.ANY)],
            out_specs=pl.BlockSpec(memory_space=pl.ANY),
            scratch_shapes=[pltpu.SemaphoreType.DMA]),
        input_output_aliases={2: 0},
        compiler_params=pltpu.CompilerParams(dimension_semantics=("arbitrary",),
                                             vmem_limit_bytes=VMEM_LIMIT, disable_bounds_checks=True),
        name="moe_dispatch",
    )(dest, h, xg0)


def _swiglu(x_glu, x_lin, alpha, limit):
    x_glu = jnp.minimum(x_glu, limit)
    x_lin = jnp.clip(x_lin, -limit, limit)
    return x_glu * jax.nn.sigmoid(alpha * x_glu) * (x_lin + 1.0)


def _moe_up_kernel(be_ref, nu_ref, x_ref, w_ref, b_ref, a_ref, *, tc, alpha, limit):
    i = pl.program_id(0)

    @pl.when(i < nu_ref[0])
    def _():
        x = x_ref[...].astype(BF16)
        for c in range(w_ref.shape[2] // tc):
            sl = slice(c * tc, (c + 1) * tc)
            hdn = jnp.dot(x, w_ref[0, :, sl], preferred_element_type=F32) + b_ref[0, :, sl]
            for j in range(tc // (2 * LANE)):
                g = hdn[:, 2 * j * LANE:(2 * j + 1) * LANE]
                lin = hdn[:, (2 * j + 1) * LANE:(2 * j + 2) * LANE]
                col = (c * tc // (2 * LANE) + j) * LANE
                a_ref[:, col:col + LANE] = _swiglu(g, lin, alpha, limit).astype(a_ref.dtype)

    @pl.when(i >= nu_ref[0])
    def _():
        a_ref[...] = jnp.zeros(a_ref.shape, a_ref.dtype)


def _moe_down_kernel(be_ref, nu_ref, a_ref, w_ref, b_ref, y_ref, w_bf16):
    i = pl.program_id(0)

    @pl.when((i == 0) | (be_ref[i] != be_ref[jnp.maximum(i - 1, 0)]))
    def _():
        w_bf16[...] = w_ref[0, 0].astype(BF16)

    @pl.when(i < nu_ref[0])
    def _():
        y_ref[...] = jnp.dot(a_ref[...], w_bf16[...], preferred_element_type=F32) + b_ref[0]

    @pl.when(i >= nu_ref[0])
    def _():
        y_ref[...] = jnp.zeros(y_ref.shape, y_ref.dtype)


def _moe_experts(blk_e, n_used, xg, w1, b1, w2_all, layer, b2, *, bm, cfg):
    d = xg.shape[1]
    n_exp, _, f2 = w1.shape
    ff = f2 // 2
    nb = blk_e.shape[0]
    p_rows = nb * bm
    tc = _divisor_tile(f2, 1024, 2 * LANE)
    a = pl.pallas_call(
        functools.partial(_moe_up_kernel, tc=tc, alpha=cfg.swiglu_alpha, limit=cfg.swiglu_limit),
        out_shape=jax.ShapeDtypeStruct((p_rows, ff), BF16),
        grid_spec=pltpu.PrefetchScalarGridSpec(
            num_scalar_prefetch=2, grid=(nb,),
            in_specs=[pl.BlockSpec((bm, d), lambda i, be, nu: (i, 0)),
                      pl.BlockSpec((1, d, f2), lambda i, be, nu: (be[i], 0, 0)),
                      pl.BlockSpec((1, 1, f2), lambda i, be, nu: (be[i], 0, 0))],
            out_specs=pl.BlockSpec((bm, ff), lambda i, be, nu: (i, 0))),
        compiler_params=_params("arbitrary"), name="moe_up_swiglu",
    )(blk_e, n_used, xg, w1, b1)
    return pl.pallas_call(
        _moe_down_kernel,
        out_shape=jax.ShapeDtypeStruct((p_rows, d), F32),
        grid_spec=pltpu.PrefetchScalarGridSpec(
            num_scalar_prefetch=2, grid=(nb,),
            in_specs=[pl.BlockSpec((bm, ff), lambda i, be, nu: (i, 0)),
                      pl.BlockSpec((1, 1, ff, d), lambda i, be, nu: (layer, be[i], 0, 0)),
                      pl.BlockSpec((1, 1, d), lambda i, be, nu: (be[i], 0, 0))],
            out_specs=pl.BlockSpec((bm, d), lambda i, be, nu: (i, 0)),
            scratch_shapes=[pltpu.VMEM((ff, d), BF16)]),
        compiler_params=_params("arbitrary"), name="moe_down",
    )(blk_e, n_used, a, w2_all, b2)


def _combine_kernel(dest_ref, y_ref, w_ref, o_ref, buf, sem, *, tm, top_k):
    base = pl.program_id(0) * tm

    def issue(c, _):
        for u in range(DMA_UNROLL):
            t = c * DMA_UNROLL + u
            for k in range(top_k):
                d = dest_ref[(base + t) * top_k + k]
                pltpu.make_async_copy(y_ref.at[pl.ds(d, 1)], buf.at[k, pl.ds(t, 1)], sem).start()
        return 0

    lax.fori_loop(0, tm // DMA_UNROLL, issue, 0)
    for k in range(top_k):
        pltpu.make_async_copy(y_ref.at[pl.ds(0, tm)], buf.at[k], sem).wait()
    w = w_ref[...]
    acc = buf[0] * w[:, 0:1]
    for k in range(1, top_k):
        acc = acc + buf[k] * w[:, k:k + 1]
    o_ref[...] = acc


def _combine(dest, y, top_w, *, tm, top_k):
    d = y.shape[1]
    n_tok, lanes = top_w.shape
    assert tm % DMA_UNROLL == 0
    return pl.pallas_call(
        functools.partial(_combine_kernel, tm=tm, top_k=top_k),
        out_shape=jax.ShapeDtypeStruct((n_tok, d), F32),
        grid_spec=pltpu.PrefetchScalarGridSpec(
            num_scalar_prefetch=1, grid=(n_tok // tm,),
            in_specs=[pl.BlockSpec(memory_space=pl.ANY),
                      pl.BlockSpec((tm, lanes), lambda i, dest: (i, 0))],
            out_specs=pl.BlockSpec((tm, d), lambda i, dest: (i, 0)),
            scratch_shapes=[pltpu.VMEM((top_k, tm, d), F32), pltpu.SemaphoreType.DMA]),
        compiler_params=pltpu.CompilerParams(dimension_semantics=("arbitrary",),
                                             vmem_limit_bytes=VMEM_LIMIT, disable_bounds_checks=True),
        name="moe_combine",
    )(dest, y, top_w)


def _moe(h, top_i, top_w, valid, w1, b1, w2_all, layer, b2, xg_init, *, cfg):
    n_tok, d = h.shape
    n_exp = w1.shape[0]
    kk = cfg.top_k
    bm = cfg.moe_rows
    idx = top_i[:, :kk]
    sel = (idx[:, :, None] == jnp.arange(n_exp, dtype=jnp.int32)[None, None, :]) & valid[:, None, None]
    sel = jnp.any(sel, axis=1).astype(jnp.int32)
    counts = jnp.sum(sel, axis=0)
    rank = jnp.cumsum(sel, axis=0) - sel
    padded = (counts + bm - 1) // bm * bm
    gend = jnp.cumsum(padded)
    gstart = gend - padded
    onehot = (idx[:, :, None] == jnp.arange(n_exp, dtype=jnp.int32)[None, None, :]).astype(jnp.int32)
    dest = jnp.sum(onehot * (gstart[None, None, :] + rank[:, None, :]), axis=-1)
    n_blocks = -(-(n_tok * kk) // bm) + n_exp
    p_rows = n_blocks * bm
    starts = jnp.arange(n_blocks, dtype=jnp.int32) * bm
    blk_e = jnp.minimum(jnp.sum(gend[None, :] <= starts[:, None], axis=1), n_exp - 1).astype(jnp.int32)
    n_used = (gend[-1] // bm).astype(jnp.int32).reshape(1)

    tmd = _divisor_tile(n_tok, cfg.dispatch_rows, 16)
    spare = p_rows + jnp.arange(n_tok * kk, dtype=jnp.int32).reshape(n_tok, kk) % (tmd * kk)
    dest_w = jnp.where(valid[:, None], dest, spare).astype(jnp.int32).reshape(-1)
    if xg_init is None:
        xg_init = jnp.zeros((p_rows + tmd * kk, d), h.dtype)
    assert xg_init.shape == (p_rows + tmd * kk, d)
    xg = _dispatch(dest_w, h, xg_init, tm=tmd, top_k=kk)
    y = _moe_experts(blk_e, n_used, xg, w1, b1, w2_all, layer, b2, bm=bm, cfg=cfg)
    any_row = jnp.arange(n_tok * kk, dtype=jnp.int32).reshape(n_tok, kk) % p_rows
    dest_r = jnp.where(valid[:, None], dest, any_row).astype(jnp.int32).reshape(-1)
    tmc = _divisor_tile(n_tok, cfg.combine_rows, 8)
    return _combine(dest_r, y, top_w, tm=tmc, top_k=kk), xg


def _final_kernel(x_ref, f_ref, gate_ref, g_ref, o_ref, *, eps):
    x = x_ref[...] + gate_ref[0] * f_ref[...]
    ms = jnp.mean(x * x, axis=-1, keepdims=True)
    o_ref[0] = x * lax.rsqrt(ms + eps) * g_ref[...]


def _final_norm(x, f, gate, g, *, batch, n_ctx, n_lat, eps):
    d = x.shape[1]
    ts = n_ctx
    tpb = (n_ctx + n_lat) // ts
    row = pl.BlockSpec((ts, d), lambda b, s: (b * tpb + 1 + s, 0))
    return pl.pallas_call(
        functools.partial(_final_kernel, eps=eps),
        out_shape=jax.ShapeDtypeStruct((batch, n_lat, d), F32),
        grid=(batch, n_lat // ts),
        in_specs=[row, row, pl.BlockSpec((1, 1, d), lambda b, s: (b, 0, 0)),
                  pl.BlockSpec((1, d), lambda b, s: (0, 0))],
        out_specs=pl.BlockSpec((1, ts, d), lambda b, s: (b, s, 0)),
        compiler_params=_params("arbitrary", "arbitrary"), name="final_norm",
    )(x, f, gate, g.reshape(1, d))


def _rope_tables(n_ctx, n_lat, grid_w, theta, dims):
    half = dims // 2
    nf = half // 2
    freqs = theta ** (-np.arange(0, half, 2, dtype=np.float64) / half)
    t = np.arange(n_lat)
    pos_r = np.concatenate([np.zeros(n_ctx), t // grid_w]).astype(np.float64)
    pos_c = np.concatenate([np.zeros(n_ctx), t % grid_w]).astype(np.float64)
    ang_r = pos_r[:, None] * freqs[None, :]
    ang_c = pos_c[:, None] * freqs[None, :]
    ang = np.concatenate([ang_r, ang_c], axis=1)
    rows = n_ctx + n_lat
    cos = np.ones((rows, LANE), np.float32)
    sin = np.zeros((rows, LANE), np.float32)
    w = 2 * nf
    cos[:, :w] = np.cos(ang)
    cos[:, LANE // 2:LANE // 2 + w] = np.cos(ang)
    sin[:, :w] = -np.sin(ang)
    sin[:, LANE // 2:LANE // 2 + w] = np.sin(ang)
    return jnp.asarray(cos), jnp.asarray(sin)


def _axial_lanes(w, dims):
    nf = dims // 4
    lead = w.shape[:-1]
    w = jnp.swapaxes(w.reshape(lead + (2, 2, nf)), -3, -2).reshape(lead + (2, 2 * nf))
    w = jnp.pad(w, [(0, 0)] * (len(lead) + 1) + [(0, LANE // 2 - 2 * nf)])
    return w.reshape(lead + (LANE,))


def _forward(x, c, ctx, c_ctx, w_mod, b_mod, g_mix, w_in, g_q_a, w_uq, g_kv_a, w_ukv, g_qn, g_kn,
             rpb, w_branch, w_out, g_ffn, w_router, b_router, w_exp1, b_exp1, w_exp2, b_exp2, g_final,
             cfg):
    batch, n_lat, d = x.shape
    n_ctx = ctx.shape[1]
    depth = w_mod.shape[0]
    t_rows = n_ctx + n_lat
    n_rows = batch * t_rows
    hd = cfg.head_dim
    q_rank, kv_rank = g_q_a.shape[1], g_kv_a.shape[1]
    ha, hb, hkv, hc = cfg.mla_heads, cfg.gqa_heads, cfg.gqa_kv_heads, cfg.na_heads
    nope, rdim, vdim = cfg.mla_nope, cfg.mla_rope, cfg.mla_v
    assert nope == LANE and vdim == LANE and hd == LANE and rdim <= LANE // 2
    assert n_lat % n_ctx == 0 and n_ctx % 16 == 0 and batch <= 7
    kh_full = (rpb.shape[2] + 1) // 2
    kw = (rpb.shape[3] + 1) // 2
    rows = n_lat // cfg.grid_w
    kh = min(kh_full, rows)
    n_branch = w_branch.shape[1]
    assert n_branch == 3

    tm = _divisor_tile(t_rows, cfg.row_tile, 16)
    tpb = t_rows // tm

    sizes = (q_rank, kv_rank, rdim, hb * hd, hkv * hd, hkv * hd, hc * hd, hc * hd, hc * hd, n_branch * d)
    offs = np.concatenate([[0], np.cumsum(sizes)])

    cos_a, sin_a = _rope_tables(n_ctx, n_lat, cfg.grid_w, cfg.rope_theta, rdim)
    cos_b, sin_b = _rope_tables(n_ctx, n_lat, cfg.grid_w, cfg.rope_theta, hd)
    rope_spec = pl.BlockSpec((tm, LANE), lambda i, j: (i % tpb, 0))
    log2e = float(np.log2(np.e))
    mla_scale = float((nope + rdim) ** -0.5) * log2e
    head_scale = float(hd ** -0.5) * log2e

    cc = jnp.zeros((8, d), F32).at[:batch].set(c).at[batch].set(c_ctx)
    mod = _modulation(cc, w_mod, b_mod)

    def table(layer, chunks):
        rows_ = []
        for ch in chunks:
            sl = mod[layer, :, ch * d:(ch + 1) * d]
            rows_.append(jnp.broadcast_to(sl[batch][None], (batch, d)))
            rows_.append(sl[:batch])
        tab = jnp.stack(rows_, axis=1)
        return jnp.pad(tab, ((0, 0), (0, 8 - tab.shape[1]), (0, 0)))

    xa = jnp.concatenate([ctx, x], axis=1).reshape(n_rows, d)
    pend_f = None
    pend_tab = None
    xg_buf = None
    is_lat = (jnp.arange(n_rows, dtype=jnp.int32) % t_rows) >= n_ctx

    for layer in range(depth):
        last = layer == depth - 1
        wi = w_in[layer]

        w_d = wi[:, offs[0]:offs[2]].astype(BF16)
        g_d = jnp.concatenate([g_q_a[layer], g_kv_a[layer]]).reshape(1, -1)
        w_kr = _axial_lanes(wi[:, offs[2]:offs[3]], rdim).astype(BF16)
        uq = w_uq[layer].reshape(q_rank, ha, nope + rdim)
        uq = jnp.concatenate([uq[:, :, :nope], _axial_lanes(uq[:, :, nope:], rdim)], axis=-1)
        uq = uq.reshape(q_rank, ha * 2 * LANE).astype(BF16)
        ukv = w_ukv[layer].reshape(kv_rank, ha, nope + vdim)
        uk = ukv[:, :, :nope].reshape(kv_rank, ha * nope).astype(BF16)
        uv = ukv[:, :, nope:].reshape(kv_rank, ha * vdim).astype(BF16)
        w_qb = _axial_lanes(wi[:, offs[3]:offs[4]].reshape(d, hb, hd), hd).reshape(d, hb * hd).astype(BF16)
        w_kb = _axial_lanes(wi[:, offs[4]:offs[5]].reshape(d, hkv, hd), hd).reshape(d, hkv * hd).astype(BF16)
        w_vb = wi[:, offs[5]:offs[6]].astype(BF16)
        g_qn_p = _axial_lanes(g_qn[layer], hd).reshape(1, hd)
        g_kn_p = _axial_lanes(g_kn[layer], hd).reshape(1, hd)
        w_c = wi[:, offs[6]:offs[9]].astype(BF16)
        c_scale = jnp.concatenate([jnp.full((hc * hd,), head_scale, F32),
                                   jnp.ones((2 * hc * hd,), F32)]).reshape(1, -1)
        w_g = wi[:, offs[9]:offs[10]].reshape(d, n_branch, d).transpose(1, 0, 2).astype(BF16)
        w_b = w_branch[layer].astype(BF16)
        w_o = w_out[layer].astype(BF16)
        ff = w_exp2.shape[2]
        w1 = _deinterleave_up_weights(w_exp1, layer)
        b1 = b_exp1[layer].reshape(-1, ff // LANE, LANE, 2).transpose(0, 1, 3, 2).reshape(-1, 1, 2 * ff)
        b2 = b_exp2[layer].reshape(-1, 1, d)

        tab1 = table(layer, (0, 1)) if pend_f is None else jnp.concatenate(
            [table(layer, (0, 1))[:, :4], pend_tab[:, :2], jnp.zeros((batch, 2, d), F32)], axis=1)
        res = _norm_mod(xa, pend_f, tab1, g_mix[layer], None, t_rows=t_rows, n_ctx=n_ctx, cfg=cfg)
        if pend_f is not None:
            xa, h = res
        else:
            (h,) = res

        dn = _matmul(h, w_d, functools.partial(_epi_rmsnorm, eps=cfg.eps),
                     [(g_d, pl.BlockSpec((1, q_rank), lambda i, j: (0, j)))],
                     jax.ShapeDtypeStruct((n_rows, q_rank + kv_rank), BF16),
                     pl.BlockSpec((tm, q_rank), lambda i, j: (i, j)), tm=tm, tn=q_rank, name="mla_down")
        assert q_rank == kv_rank
        k_pe = _matmul(h, w_kr,
                       functools.partial(_epi_rope_tiles, rope_tiles=(True,), norm=False, eps=cfg.eps, scale=1.0),
                       [(cos_a, rope_spec), (sin_a, rope_spec)],
                       jax.ShapeDtypeStruct((n_rows, LANE), BF16),
                       pl.BlockSpec((tm, LANE), lambda i, j: (i, 0)), tm=tm, tn=LANE, name="mla_rope_key")
        hpt = 2 if ha % 2 == 0 else 1
        q_a = _matmul(dn, uq,
                      functools.partial(_epi_rope_tiles, rope_tiles=(False, True), norm=False, eps=cfg.eps,
                                        scale=mla_scale),
                      [(cos_a, rope_spec), (sin_a, rope_spec)],
                      jax.ShapeDtypeStruct((n_rows, ha * 2 * LANE), BF16),
                      pl.BlockSpec((tm, hpt * 2 * LANE), lambda i, j: (i, j)), tm=tm, tn=hpt * 2 * LANE, a_col=0,
                      name="mla_q_up")
        k_a = _matmul(dn, uk, _epi_mla_key,
                      [(k_pe, pl.BlockSpec((tm, LANE), lambda i, j: (i, 0)))],
                      jax.ShapeDtypeStruct((n_rows, ha * 2 * LANE), BF16),
                      pl.BlockSpec((tm, hpt * 2 * LANE), lambda i, j: (i, j)), tm=tm, tn=hpt * LANE, a_col=1,
                      name="mla_k_up")
        tn_v = _divisor_tile(ha * vdim, 4 * LANE, LANE)
        v_a = _matmul(dn, uv, _epi_plain, [],
                      jax.ShapeDtypeStruct((n_rows, ha * vdim), BF16),
                      pl.BlockSpec((tm, tn_v), lambda i, j: (i, j)), tm=tm, tn=tn_v, a_col=1, name="mla_v_up")

        tn_q = _divisor_tile(hb * hd, 4 * LANE, LANE)
        q_b = _matmul(h, w_qb,
                      functools.partial(_epi_rope_tiles, rope_tiles=(True,), norm=True, eps=cfg.eps,
                                        scale=head_scale),
                      [(g_qn_p, pl.BlockSpec((1, hd), lambda i, j: (0, 0))), (cos_b, rope_spec), (sin_b, rope_spec)],
                      jax.ShapeDtypeStruct((n_rows, hb * hd), BF16),
                      pl.BlockSpec((tm, tn_q), lambda i, j: (i, j)), tm=tm, tn=tn_q, name="gqa_q")
        tn_k = _divisor_tile(hkv * hd, 4 * LANE, LANE)
        k_b = _matmul(h, w_kb,
                      functools.partial(_epi_rope_tiles, rope_tiles=(True,), norm=True, eps=cfg.eps, scale=1.0),
                      [(g_kn_p, pl.BlockSpec((1, hd), lambda i, j: (0, 0))), (cos_b, rope_spec), (sin_b, rope_spec)],
                      jax.ShapeDtypeStruct((n_rows, hkv * hd), BF16),
                      pl.BlockSpec((tm, tn_k), lambda i, j: (i, j)), tm=tm, tn=tn_k, name="gqa_k")
        v_b = _matmul(h, w_vb, _epi_plain, [],
                      jax.ShapeDtypeStruct((n_rows, hkv * hd), BF16),
                      pl.BlockSpec((tm, tn_k), lambda i, j: (i, j)), tm=tm, tn=tn_k, name="gqa_v")

        tn_c = _divisor_tile(3 * hc * hd, 4 * LANE, LANE)
        qkv_c = _matmul(h, w_c, _epi_scale, [(c_scale, pl.BlockSpec((1, tn_c), lambda i, j: (0, j)))],
                        jax.ShapeDtypeStruct((n_rows, 3 * hc * hd), BF16),
                        pl.BlockSpec((tm, tn_c), lambda i, j: (i, j)), tm=tm, tn=tn_c, name="natten_qkv")

        common = dict(batch=batch, n_ctx=n_ctx, n_lat=n_lat, cfg=cfg)
        assert ha * vdim == hb * hd == hc * hd
        o_a = o_b = o_c = jnp.zeros((n_rows, hc * hd), BF16)
        spec_a = dict(heads=ha, kv_heads=ha, dk=2 * LANE, dv=vdim, q_col=0, k_col=0, v_col=0)
        spec_b = dict(heads=hb, kv_heads=hkv, dk=hd, dv=hd, q_col=0, k_col=0, v_col=0)
        if not last:
            o_a = _attention(q_a, k_a, v_a, o_a, q_tiles="ctx", name="mla_ctx_attention", **spec_a, **common)
            o_b = _attention(q_b, k_b, v_b, o_b, q_tiles="ctx", name="gqa_ctx_attention", **spec_b, **common)
            o_c = _attention(qkv_c, qkv_c, qkv_c, o_c, heads=hc, kv_heads=hc, dk=hd, dv=hd,
                             q_col=0, k_col=hc, v_col=2 * hc, q_tiles="ctx", name="natten_ctx_attention",
                             **common)
        o_a = _attention(q_a, k_a, v_a, o_a, q_tiles="lat", name="mla_attention", **spec_a, **common)
        o_b = _attention(q_b, k_b, v_b, o_b, q_tiles="lat", name="gqa_attention", **spec_b, **common)
        bias, group_type = _natten_bias(rpb[layer], rows, cfg.grid_w, kh, kw, n_ctx // cfg.grid_w, log2e)
        o_c = _natten(qkv_c, o_c, bias, group_type, batch=batch, n_ctx=n_ctx, n_lat=n_lat, heads=hc, hd=hd,
                      grid_w=cfg.grid_w, kh=kh, name="natten")

        tn_m = _divisor_tile(d, 4 * LANE, LANE)
        merged = _merge(h, o_a, o_b, o_c, w_g, w_b, tm=tm, tn=tn_m)
        gate1 = table(layer, (2,))
        x1 = _matmul(merged, w_o, functools.partial(_epi_residual, tiles_per_batch=tpb, tm=tm, n_ctx=n_ctx),
                     [(xa, pl.BlockSpec((tm, tn_m), lambda i, j: (i, j))),
                      (gate1, pl.BlockSpec((1, 8, tn_m), lambda i, j: (i // tpb, 0, j)))],
                     jax.ShapeDtypeStruct((n_rows, d), F32),
                     pl.BlockSpec((tm, tn_m), lambda i, j: (i, j)), tm=tm, tn=tn_m, name="out_proj_residual")

        h2, top_i, top_w = _norm_mod(x1, None, table(layer, (3, 4)), g_ffn[layer],
                                     (w_router[layer], b_router[layer]), t_rows=t_rows, n_ctx=n_ctx, cfg=cfg)
        valid = is_lat if last else jnp.ones((n_rows,), bool)
        pend_f, xg_buf = _moe(h2, top_i, top_w, valid, w1, b1, w_exp2, layer, b2, xg_buf, cfg=cfg)
        pend_tab = table(layer, (5,))
        xa = x1

    gate2 = pend_tab[:, 1:2]
    return _final_norm(xa, pend_f, gate2, g_final, batch=batch, n_ctx=n_ctx, n_lat=n_lat, eps=cfg.eps)


def kernel(x, c, ctx, c_ctx, w_mod, b_mod, g_mix, w_in, g_q_a, w_uq, g_kv_a, w_ukv, g_qn, g_kn, rpb,
           w_branch, w_out, g_ffn, w_router, b_router, w_exp1, b_exp1, w_exp2, b_exp2, g_final):
    return _forward(x, c, ctx, c_ctx, w_mod, b_mod, g_mix, w_in, g_q_a, w_uq, g_kv_a, w_ukv, g_qn, g_kn,
                    rpb, w_branch, w_out, g_ffn, w_router, b_router, w_exp1, b_exp1, w_exp2, b_exp2,
                    g_final, CFG)
```
